```python
import math
import jax, jax.numpy as jnp
from jax import lax
import numpy as np

D_MODEL = 4096
BATCH = 4
SEQ = 2048
DEPTH = 2
DEC_BATCH = 128
DEC_SEQ = 1
PAST_LEN = 16384
PAGE_SIZE = 128

EPS = 1e-6
ROPE_THETA = 500000.0
FFN_RES_WEIGHT = 0.5
D_FF = 2 * D_MODEL
MIX_WIDTH = D_MODEL
Q_BLOCK = 128
MLA_NOPE = 128
MLA_ROPE = 64
MLA_V = 128
MLA_HEADS = (3 * MIX_WIDTH // 8) // MLA_V
MLA_Q_LORA = 3 * D_MODEL // 16
MLA_KV_LORA = 3 * D_MODEL // 64
MOBA_HD = 64
MOBA_HEADS = (5 * MIX_WIDTH // 16) // MOBA_HD
MOBA_KV_HEADS = 1
MOBA_REP = MOBA_HEADS // MOBA_KV_HEADS
MOBA_BLOCK = 256
MOBA_TOPK = 3
MOBA_QB = 16
MOBA_ROT = MOBA_HD // 4
DIFF_D = 64
DIFF_HEADS = (MIX_WIDTH - MLA_HEADS * MLA_V - MOBA_HEADS * MOBA_HD) // (2 * DIFF_D)
DIFF_KV_HEADS = 1
DIFF_REP = DIFF_HEADS // DIFF_KV_HEADS
DIFF_ROT = DIFF_D // 4
IN_SPLITS = (MLA_Q_LORA, MLA_KV_LORA, MLA_ROPE,
             MOBA_HEADS * MOBA_HD, MOBA_KV_HEADS * MOBA_HD, MOBA_KV_HEADS * MOBA_HD,
             DIFF_HEADS * 2 * DIFF_D, DIFF_KV_HEADS * 2 * DIFF_D, DIFF_KV_HEADS * 2 * DIFF_D)
IN_WIDTH = (MLA_Q_LORA + MLA_KV_LORA + MLA_ROPE + (MOBA_HEADS + 2 * MOBA_KV_HEADS) * MOBA_HD
            + (DIFF_HEADS + 2 * DIFF_KV_HEADS) * 2 * DIFF_D)

kernel_name = 'hymba_mla_moba_diff_macaron_step'


def rms_norm(x, g):
    xf = x.astype(jnp.float32)
    y = xf * lax.rsqrt(jnp.mean(xf * xf, axis=-1, keepdims=True) + EPS)
    return (y * g.astype(jnp.float32)).astype(x.dtype)


def swiglu(x, wg, wu, wd):
    return (jax.nn.silu(x @ wg) * (x @ wu)) @ wd


def apply_rope(x, pos, rot_dim):
    half = rot_dim // 2
    inv = ROPE_THETA ** (-jnp.arange(half, dtype=jnp.float32) / half)
    ang = pos.astype(jnp.float32)[:, None] * inv[None, :]
    ang = ang.reshape((ang.shape[0],) + (1,) * (x.ndim - 3) + (half,))
    cos, sin = jnp.cos(ang).astype(x.dtype), jnp.sin(ang).astype(x.dtype)
    x1, x2, rest = x[..., :half], x[..., half:rot_dim], x[..., rot_dim:]
    return jnp.concatenate([x1 * cos - x2 * sin, x2 * cos + x1 * sin, rest], axis=-1)


def split_cols(a, sizes):
    outs, s = [], 0
    for n in sizes:
        outs.append(a[..., s:s + n])
        s += n
    return outs


def q_block(n, pref):
    return pref if n % pref == 0 else n


def sweep_queries(fn, n_q, qb):
    out = lax.map(fn, jnp.arange(n_q // qb, dtype=jnp.int32) * qb)
    out = jnp.moveaxis(out, 0, 1)
    return out.reshape((out.shape[0], n_q) + out.shape[3:])


def mla_attend(q_lat, q_rope, ckv, krope, offset):
    Sq = q_lat.shape[1]
    k_pos = jnp.arange(ckv.shape[1])
    scale = (MLA_NOPE + MLA_ROPE) ** -0.5
    qb = q_block(Sq, Q_BLOCK)

    def blk(s):
        ql = lax.dynamic_slice_in_dim(q_lat, s, qb, axis=1)
        qr = lax.dynamic_slice_in_dim(q_rope, s, qb, axis=1)
        sc = (jnp.einsum('bqhc,btc->bhqt', ql, ckv).astype(jnp.float32)
              + jnp.einsum('bqhr,btr->bhqt', qr, krope).astype(jnp.float32)) * scale
        q_pos = offset + s + jnp.arange(qb)
        sc = jnp.where(k_pos[None, :] <= q_pos[:, None], sc, -jnp.inf)
        p = jax.nn.softmax(sc, axis=-1).astype(ckv.dtype)
        return jnp.einsum('bhqt,btc->bqhc', p, ckv)

    return sweep_queries(blk, Sq, qb)


def moba_attend(q, k, v, offset):
    B, Sq, G, R, hd = q.shape
    T = k.shape[1]
    nb = max(-(-T // MOBA_BLOCK), MOBA_TOPK)
    pad = nb * MOBA_BLOCK - T
    kp = jnp.pad(k, ((0, 0), (0, pad), (0, 0), (0, 0)))
    vp = jnp.pad(v, ((0, 0), (0, pad), (0, 0), (0, 0)))
    kb = kp.reshape(B, nb, MOBA_BLOCK, G, hd).transpose(0, 3, 1, 2, 4)
    vb = vp.reshape(B, nb, MOBA_BLOCK, G, hd).transpose(0, 3, 1, 2, 4)
    kmean = jnp.mean(kb.astype(jnp.float32), axis=3)
    blk_ids = jnp.arange(nb)
    bidx = jnp.arange(B)[:, None, None, None, None]
    gidx = jnp.arange(G)[None, None, :, None, None]
    scale = hd ** -0.5
    qb = q_block(Sq, MOBA_QB)
    n_sel = MOBA_TOPK * MOBA_BLOCK

    def blk(s):
        qs = lax.dynamic_slice_in_dim(q, s, qb, axis=1)
        q_pos = offset + s + jnp.arange(qb)
        cur_q = q_pos // MOBA_BLOCK
        gs = jnp.einsum('bqgrd,bgnd->bqgrn', qs.astype(jnp.float32), kmean)
        gs = jnp.where((blk_ids[None, :] < cur_q[:, None])[:, None, None, :], gs, -jnp.inf)
        _, sel = lax.top_k(gs, MOBA_TOPK)
        valid = sel < cur_q[:, None, None, None]
        ksel = kb[bidx, gidx, sel]
        vsel = vb[bidx, gidx, sel]
        s_sel = jnp.einsum('bqgrd,bqgrknd->bqgrkn', qs, ksel).astype(jnp.float32) * scale
        s_sel = jnp.where(valid[..., None], s_sel, -jnp.inf)
        start = ((offset + s) // MOBA_BLOCK) * MOBA_BLOCK
        kl = lax.dynamic_slice_in_dim(kp, start, MOBA_BLOCK, axis=1)
        vl = lax.dynamic_slice_in_dim(vp, start, MOBA_BLOCK, axis=1)
        s_loc = jnp.einsum('bqgrd,bngd->bqgrn', qs, kl).astype(jnp.float32) * scale
        k_pos = start + jnp.arange(MOBA_BLOCK)
        s_loc = jnp.where((k_pos[None, :] <= q_pos[:, None])[:, None, None, :], s_loc, -jnp.inf)
        sc = jnp.concatenate([s_sel.reshape(B, qb, G, R, n_sel), s_loc], axis=-1)
        p = jax.nn.softmax(sc, axis=-1).astype(v.dtype)
        p_sel = p[..., :n_sel].reshape(B, qb, G, R, MOBA_TOPK, MOBA_BLOCK)
        return (jnp.einsum('bqgrkn,bqgrknd->bqgrd', p_sel, vsel)
                + jnp.einsum('bqgrn,bngd->bqgrd', p[..., n_sel:], vl))

    return sweep_queries(blk, Sq, qb)


def diff_attend(q, k, v, lam, offset):
    Sq = q.shape[1]
    k_pos = jnp.arange(k.shape[1])
    scale = DIFF_D ** -0.5
    qb = q_block(Sq, Q_BLOCK)

    def blk(s):
        qs = lax.dynamic_slice_in_dim(q, s, qb, axis=1)
        sc = jnp.einsum('bqgrcd,btgcd->bgrcqt', qs, k).astype(jnp.float32) * scale
        q_pos = offset + s + jnp.arange(qb)
        sc = jnp.where(k_pos[None, :] <= q_pos[:, None], sc, -jnp.inf)
        p = jax.nn.softmax(sc, axis=-1)
        w = (p[:, :, :, 0] - lam * p[:, :, :, 1]).astype(v.dtype)
        return jnp.einsum('bgrqt,btgv->bqgrv', w, v)

    return sweep_queries(blk, Sq, qb)


def mla_mixer(l, cq, ckv, kr, pos, offset, past, p):
    B, Sq = cq.shape[:2]
    q = (rms_norm(cq, p['mla_q_norm_g'][l]) @ p['mla_w_uq'][l]).reshape(B, Sq, MLA_HEADS, MLA_NOPE + MLA_ROPE)
    q_nope = q[..., :MLA_NOPE]
    q_rope = apply_rope(q[..., MLA_NOPE:], pos, MLA_ROPE)
    ckv = rms_norm(ckv, p['mla_kv_norm_g'][l])
    kr = apply_rope(kr, pos, MLA_ROPE)
    w_ukv = p['mla_w_ukv'][l].reshape(MLA_KV_LORA, MLA_HEADS, MLA_NOPE + MLA_V)
    q_lat = jnp.einsum('bshd,chd->bshc', q_nope, w_ukv[..., :MLA_NOPE])
    o_lat = mla_attend(q_lat, q_rope, past(ckv, 0), past(kr, 1), offset)
    o = jnp.einsum('bshc,chd->bshd', o_lat, w_ukv[..., MLA_NOPE:])
    return o.reshape(B, Sq, MLA_HEADS * MLA_V), ckv, kr


def moba_mixer(mq, mk, mv, pos, offset, past):
    B, Sq = mq.shape[:2]
    q = apply_rope(mq.reshape(B, Sq, MOBA_KV_HEADS, MOBA_REP, MOBA_HD), pos, MOBA_ROT)
    k = apply_rope(mk.reshape(B, Sq, MOBA_KV_HEADS, MOBA_HD), pos, MOBA_ROT)
    v = mv.reshape(B, Sq, MOBA_KV_HEADS, MOBA_HD)
    o = moba_attend(q, past(k, 2), past(v, 3), offset)
    return o.reshape(B, Sq, MOBA_HEADS * MOBA_HD), k, v


def diff_mixer(l, dq, dk, dv, pos, offset, past, p):
    B, Sq = dq.shape[:2]
    q = apply_rope(dq.reshape(B, Sq, DIFF_KV_HEADS, DIFF_REP, 2, DIFF_D), pos, DIFF_ROT)
    k = apply_rope(dk.reshape(B, Sq, DIFF_KV_HEADS, 2, DIFF_D), pos, DIFF_ROT)
    v = dv.reshape(B, Sq, DIFF_KV_HEADS, 2 * DIFF_D)
    lam_init = 0.8 - 0.6 * math.exp(-0.3 * l)
    f32 = jnp.float32
    lam = (jnp.exp(jnp.sum(p['diff_lambda_q1'][l].astype(f32) * p['diff_lambda_k1'][l].astype(f32)))
           - jnp.exp(jnp.sum(p['diff_lambda_q2'][l].astype(f32) * p['diff_lambda_k2'][l].astype(f32)))
           + lam_init)
    o = diff_attend(q, past(k, 4), past(v, 5), lam, offset)
    o = rms_norm(o, p['diff_subln_g'][l]) * (1.0 - lam_init)
    return o.reshape(B, Sq, DIFF_HEADS * 2 * DIFF_D), k, v


def token_mix(l, h, caches, page_table, p):
    B, Sq, _ = h.shape
    offset = 0 if caches is None else PAST_LEN
    pos = offset + jnp.arange(Sq)

    def past(new, i):
        if caches is None:
            return new
        g = caches[i][l, page_table]
        g = g.reshape((B, g.shape[1] * g.shape[2]) + g.shape[3:])
        return jnp.concatenate([g.astype(new.dtype), new], axis=1)

    cq, ckv, kr, mq, mk, mv, dq, dk, dv = split_cols(h @ p['w_in'][l], IN_SPLITS)
    o_a, r_ckv, r_kr = mla_mixer(l, cq, ckv, kr, pos, offset, past, p)
    o_b, r_mk, r_mv = moba_mixer(mq, mk, mv, pos, offset, past)
    o_c, r_dk, r_dv = diff_mixer(l, dq, dk, dv, pos, offset, past, p)
    out = jnp.concatenate([o_a, o_b, o_c], axis=-1) @ p['w_out'][l]
    return out, (r_ckv, r_kr, r_mk, r_mv, r_dk, r_dv)


def run_trunk(x, caches, page_table, p):
    rows = ([], [], [], [], [], [])
    for l in range(DEPTH):
        x = x + FFN_RES_WEIGHT * swiglu(rms_norm(x, p['ffn1_norm_g'][l]), p['ffn1_w_gate'][l],
                                        p['ffn1_w_up'][l], p['ffn1_w_down'][l])
        mixed, new = token_mix(l, rms_norm(x, p['mix_norm_g'][l]), caches, page_table, p)
        x = x + mixed
        x = x + FFN_RES_WEIGHT * swiglu(rms_norm(x, p['ffn2_norm_g'][l]), p['ffn2_w_gate'][l],
                                        p['ffn2_w_up'][l], p['ffn2_w_down'][l])
        for acc, r in zip(rows, new):
            acc.append(r)
    y = rms_norm(x, p['final_norm_g'])
    ckv, kr, mk, mv, dk, dv = [jnp.stack(a) for a in rows]
    return y, ckv, kr, mk, mv, dk, dv


def setup_inputs(seed: int = 0) -> dict:
    key = jax.random.key(seed)
    k = jax.random.split(key, 32)
    f32 = jnp.float32

    def nrm(i, shape, scale=1.0):
        return scale * jax.random.normal(k[i], shape, f32)

    def gain(i, shape):
        return 1.0 + 0.02 * jax.random.normal(k[i], shape, f32)

    n_pages = PAST_LEN // PAGE_SIZE
    n_used = DEC_BATCH * n_pages
    n_pool = n_used + n_used // 4
    perm = jax.random.permutation(k[0], n_pool)
    page_table = perm[:n_used].reshape(DEC_BATCH, n_pages).astype(jnp.int32)
    return {
        'x_prompt': nrm(1, (BATCH, SEQ, D_MODEL)),
        'x_sample': nrm(2, (DEC_BATCH, DEC_SEQ, D_MODEL)),
        'cache_mla_ckv': nrm(3, (DEPTH, n_pool, PAGE_SIZE, MLA_KV_LORA)),
        'cache_mla_krope': nrm(4, (DEPTH, n_pool, PAGE_SIZE, MLA_ROPE)),
        'cache_moba_k': nrm(5, (DEPTH, n_pool, PAGE_SIZE, MOBA_KV_HEADS, MOBA_HD)),
        'cache_moba_v': nrm(6, (DEPTH, n_pool, PAGE_SIZE, MOBA_KV_HEADS, MOBA_HD)),
        'cache_diff_k': nrm(7, (DEPTH, n_pool, PAGE_SIZE, DIFF_KV_HEADS, 2, DIFF_D)),
        'cache_diff_v': nrm(8, (DEPTH, n_pool, PAGE_SIZE, DIFF_KV_HEADS, 2 * DIFF_D)),
        'page_table': page_table,
        'ffn1_norm_g': gain(9, (DEPTH, D_MODEL)),
        'ffn1_w_gate': nrm(10, (DEPTH, D_MODEL, D_FF), D_MODEL ** -0.5),
        'ffn1_w_up': nrm(11, (DEPTH, D_MODEL, D_FF), D_MODEL ** -0.5),
        'ffn1_w_down': nrm(12, (DEPTH, D_FF, D_MODEL), D_FF ** -0.5),
        'mix_norm_g': gain(13, (DEPTH, D_MODEL)),
        'w_in': nrm(14, (DEPTH, D_MODEL, IN_WIDTH), D_MODEL ** -0.5),
        'mla_q_norm_g': gain(15, (DEPTH, MLA_Q_LORA)),
        'mla_w_uq': nrm(16, (DEPTH, MLA_Q_LORA, MLA_HEADS * (MLA_NOPE + MLA_ROPE)), MLA_Q_LORA ** -0.5),
        'mla_kv_norm_g': gain(17, (DEPTH, MLA_KV_LORA)),
        'mla_w_ukv': nrm(18, (DEPTH, MLA_KV_LORA, MLA_HEADS * (MLA_NOPE + MLA_V)), MLA_KV_LORA ** -0.5),
        'diff_lambda_q1': nrm(19, (DEPTH, DIFF_D), 0.1),
        'diff_lambda_k1': nrm(20, (DEPTH, DIFF_D), 0.1),
        'diff_lambda_q2': nrm(21, (DEPTH, DIFF_D), 0.1),
        'diff_lambda_k2': nrm(22, (DEPTH, DIFF_D), 0.1),
        'diff_subln_g': gain(23, (DEPTH, 2 * DIFF_D)),
        'w_out': nrm(24, (DEPTH, MIX_WIDTH, D_MODEL), MIX_WIDTH ** -0.5),
        'ffn2_norm_g': gain(25, (DEPTH, D_MODEL)),
        'ffn2_w_gate': nrm(26, (DEPTH, D_MODEL, D_FF), D_MODEL ** -0.5),
        'ffn2_w_up': nrm(27, (DEPTH, D_MODEL, D_FF), D_MODEL ** -0.5),
        'ffn2_w_down': nrm(28, (DEPTH, D_FF, D_MODEL), D_FF ** -0.5),
        'final_norm_g': gain(29, (D_MODEL,)),
    }


def reference(x_prompt, x_sample, cache_mla_ckv, cache_mla_krope, cache_moba_k, cache_moba_v,
              cache_diff_k, cache_diff_v, page_table, ffn1_norm_g, ffn1_w_gate, ffn1_w_up, ffn1_w_down,
              mix_norm_g, w_in, mla_q_norm_g, mla_w_uq, mla_kv_norm_g, mla_w_ukv,
              diff_lambda_q1, diff_lambda_k1, diff_lambda_q2, diff_lambda_k2, diff_subln_g, w_out,
              ffn2_norm_g, ffn2_w_gate, ffn2_w_up, ffn2_w_down, final_norm_g):
    p = {
        'ffn1_norm_g': ffn1_norm_g, 'ffn1_w_gate': ffn1_w_gate, 'ffn1_w_up': ffn1_w_up,
        'ffn1_w_down': ffn1_w_down, 'mix_norm_g': mix_norm_g, 'w_in': w_in,
        'mla_q_norm_g': mla_q_norm_g, 'mla_w_uq': mla_w_uq, 'mla_kv_norm_g': mla_kv_norm_g,
        'mla_w_ukv': mla_w_ukv, 'diff_lambda_q1': diff_lambda_q1, 'diff_lambda_k1': diff_lambda_k1,
        'diff_lambda_q2': diff_lambda_q2, 'diff_lambda_k2': diff_lambda_k2, 'diff_subln_g': diff_subln_g,
        'w_out': w_out, 'ffn2_norm_g': ffn2_norm_g, 'ffn2_w_gate': ffn2_w_gate, 'ffn2_w_up': ffn2_w_up,
        'ffn2_w_down': ffn2_w_down, 'final_norm_g': final_norm_g,
    }
    caches = (cache_mla_ckv, cache_mla_krope, cache_moba_k, cache_moba_v, cache_diff_k, cache_diff_v)
    y_prompt, p_ckv, p_kr, p_mk, p_mv, p_dk, p_dv = run_trunk(x_prompt, None, None, p)
    y_sample, s_ckv, s_kr, s_mk, s_mv, s_dk, s_dv = run_trunk(x_sample, caches, page_table, p)
    return (y_prompt, y_sample, p_ckv, p_kr, p_mk, p_mv, p_dk, p_dv,
            s_ckv, s_kr, s_mk, s_mv, s_dk, s_dv)
```

```python
import functools
import math

import jax
import jax.numpy as jnp
from jax import lax
from jax.experimental import pallas as pl
from jax.experimental.pallas import tpu as pltpu

F32 = jnp.float32
BF16 = jnp.bfloat16

EPS = 1e-6
ROPE_THETA = 500000.0
FFN_RES_WEIGHT = 0.5
MLA_HEADS = 12
MLA_NOPE = 128
MLA_ROPE = 64
MLA_LORA = 192
MLA_SCALE = (MLA_NOPE + MLA_ROPE) ** -0.5
MOBA_HEADS = 20
MOBA_HD = 64
MOBA_BLOCK = 256
MOBA_TOPK = 3
DIFF_HEADS = 10
DIFF_D = 64
QK_SCALE_64 = 0.125
PAGE = 128
PAGES_PER_STEP = 16
LANES = 128
VMEM_LIMIT = 56 * 1024 * 1024

NT_DIMS = (((1,), (1,)), ((), ()))


def _cparams(sem):
    return pltpu.CompilerParams(dimension_semantics=sem, vmem_limit_bytes=VMEM_LIMIT)


def _rmsnorm_kernel(x_ref, g_ref, o_ref):
    x = x_ref[...]
    y = x * lax.rsqrt(jnp.mean(x * x, axis=-1, keepdims=True) + EPS)
    o_ref[...] = (y * g_ref[...]).astype(o_ref.dtype)


def rmsnorm(x, g, out_dtype, tm):
    n, d = x.shape
    return pl.pallas_call(
        _rmsnorm_kernel,
        grid=(n // tm,),
        in_specs=[pl.BlockSpec((tm, d), lambda i: (i, 0)),
                  pl.BlockSpec((1, d), lambda i: (0, 0))],
        out_specs=pl.BlockSpec((tm, d), lambda i: (i, 0)),
        out_shape=jax.ShapeDtypeStruct((n, d), out_dtype),
        compiler_params=_cparams(("parallel",)),
        name="rmsnorm",
    )(x, g.reshape(1, d))


def _gateup_kernel(x_ref, wg_ref, wu_ref, o_ref):
    x = x_ref[...]
    g = jnp.dot(x, wg_ref[...], preferred_element_type=F32)
    u = jnp.dot(x, wu_ref[...], preferred_element_type=F32)
    o_ref[...] = (g * jax.nn.sigmoid(g) * u).astype(o_ref.dtype)


def ffn_gateup(xn, wg, wu, tm, tn):
    n, d = xn.shape
    f = wg.shape[1]
    return pl.pallas_call(
        _gateup_kernel,
        grid=(n // tm, f // tn),
        in_specs=[pl.BlockSpec((tm, d), lambda i, j: (i, 0)),
                  pl.BlockSpec((d, tn), lambda i, j: (0, j)),
                  pl.BlockSpec((d, tn), lambda i, j: (0, j))],
        out_specs=pl.BlockSpec((tm, tn), lambda i, j: (i, j)),
        out_shape=jax.ShapeDtypeStruct((n, f), BF16),
        compiler_params=_cparams(("parallel", "arbitrary")),
        name="ffn_gateup",
    )(xn, wg, wu)


def _down_kernel(h_ref, w_ref, x_ref, o_ref):
    y = jnp.dot(h_ref[...], w_ref[...], preferred_element_type=F32)
    o_ref[...] = x_ref[...] + FFN_RES_WEIGHT * y


def ffn_down(h, wd, x, tm, tn):
    n, f = h.shape
    d = wd.shape[1]
    return pl.pallas_call(
        _down_kernel,
        grid=(n // tm, d // tn),
        in_specs=[pl.BlockSpec((tm, f), lambda i, j: (i, 0)),
                  pl.BlockSpec((f, tn), lambda i, j: (0, j)),
                  pl.BlockSpec((tm, tn), lambda i, j: (i, j))],
        out_specs=pl.BlockSpec((tm, tn), lambda i, j: (i, j)),
        out_shape=jax.ShapeDtypeStruct((n, d), F32),
        compiler_params=_cparams(("parallel", "arbitrary")),
        name="ffn_down",
    )(h, wd, x)


def _matmul_kernel(x_ref, w_ref, o_ref):
    o_ref[...] = jnp.dot(x_ref[...], w_ref[...], preferred_element_type=F32)


def matmul_f32out(x, w, tm, tn):
    n, d = x.shape
    f = w.shape[1]
    return pl.pallas_call(
        _matmul_kernel,
        grid=(n // tm, f // tn),
        in_specs=[pl.BlockSpec((tm, d), lambda i, j: (i, 0)),
                  pl.BlockSpec((d, tn), lambda i, j: (0, j))],
        out_specs=pl.BlockSpec((tm, tn), lambda i, j: (i, j)),
        out_shape=jax.ShapeDtypeStruct((n, f), F32),
        compiler_params=_cparams(("parallel", "arbitrary")),
        name="in_proj",
    )(x, w)


def _outproj_kernel(a_ref, b_ref, c_ref, wa_ref, wb_ref, wc_ref, x_ref, o_ref):
    y = jnp.dot(a_ref[...], wa_ref[...], preferred_element_type=F32)
    y = y + jnp.dot(b_ref[...], wb_ref[...], preferred_element_type=F32)
    y = y + jnp.dot(c_ref[...], wc_ref[...], preferred_element_type=F32)
    o_ref[...] = x_ref[...] + y


def out_proj(a, b, c, wa, wb, wc, x, tm, tn):
    n = a.shape[0]
    d = wa.shape[1]
    act = lambda k: pl.BlockSpec((tm, k), lambda i, j: (i, 0))
    wsp = lambda k: pl.BlockSpec((k, tn), lambda i, j: (0, j))
    return pl.pallas_call(
        _outproj_kernel,
        grid=(n // tm, d // tn),
        in_specs=[act(a.shape[1]), act(b.shape[1]), act(c.shape[1]),
                  wsp(wa.shape[0]), wsp(wb.shape[0]), wsp(wc.shape[0]),
                  pl.BlockSpec((tm, tn), lambda i, j: (i, j))],
        out_specs=pl.BlockSpec((tm, tn), lambda i, j: (i, j)),
        out_shape=jax.ShapeDtypeStruct((n, d), F32),
        compiler_params=_cparams(("parallel", "arbitrary")),
        name="out_proj",
    )(a, b, c, wa, wb, wc, x)


_C_CQ, _C_KV, _C_MQ, _C_MKV, _C_DQ, _C_DK, _C_DV = 0, 768, 1024, 2304, 2432, 3712, 3840
_Q_ROPE0 = MLA_HEADS * MLA_NOPE


def _rot(blk, c, sa, sb, half):
    return (blk * c + pltpu.roll(blk, LANES - half, 1) * sa + pltpu.roll(blk, half, 1) * sb)


def _post_kernel(z_ref, tab_ref, gq_ref, gkv_ref, wuq_ref, wk_ref,
                 qm_ref, kv_ref, mq_ref, mkv_ref, dq_ref, dkv_ref):
    c16, a16, b16 = tab_ref[:, 0:128], tab_ref[:, 128:256], tab_ref[:, 256:384]
    c64, a64, b64 = tab_ref[:, 384:512], tab_ref[:, 512:640], tab_ref[:, 640:768]
    rot16 = lambda blk: _rot(blk, c16, a16, b16, 8)
    rot64 = lambda blk: _rot(blk, c64, a64, b64, 32)
    lane = lax.broadcasted_iota(jnp.int32, (1, LANES), 1)
    low = lane < 64

    cq = z_ref[:, _C_CQ:_C_CQ + 768]
    cqn = cq * lax.rsqrt(jnp.mean(cq * cq, axis=-1, keepdims=True) + EPS) * gq_ref[...]
    q = jnp.dot(cqn.astype(BF16), wuq_ref[...], preferred_element_type=F32)
    for h in range(MLA_HEADS):
        ql = jnp.dot(q[:, 128 * h:128 * (h + 1)].astype(BF16), wk_ref[h],
                     preferred_element_type=F32)
        rb = rot64(q[:, _Q_ROPE0 + 128 * h:_Q_ROPE0 + 128 * (h + 1)])
        qm_ref[:, 256 * h:256 * h + 128] = (ql[:, :128] * MLA_SCALE).astype(BF16)
        qm_ref[:, 256 * h + 128:256 * h + 256] = ((ql[:, 128:] + rb) * MLA_SCALE).astype(BF16)

    k0 = z_ref[:, _C_KV:_C_KV + 128]
    k1 = z_ref[:, _C_KV + 128:_C_KV + 256]
    ss = (jnp.sum(k0 * k0, axis=-1, keepdims=True)
          + jnp.sum(jnp.where(low, k1 * k1, 0.0), axis=-1, keepdims=True))
    r = lax.rsqrt(ss * (1.0 / MLA_LORA) + EPS)
    kv_ref[:, 0:128] = k0 * r * gkv_ref[:, 0:128]
    kv_ref[:, 128:256] = jnp.where(low, k1 * r * gkv_ref[:, 128:256], rot64(k1))

    for j in range(MOBA_HEADS // 2):
        mq_ref[:, 128 * j:128 * (j + 1)] = rot16(z_ref[:, _C_MQ + 128 * j:_C_MQ + 128 * (j + 1)])
    mkv = z_ref[:, _C_MKV:_C_MKV + 128]
    mkv_ref[...] = jnp.where(low, rot16(mkv), mkv)

    for j in range(DIFF_HEADS):
        blk = rot16(z_ref[:, _C_DQ + 128 * j:_C_DQ + 128 * (j + 1)])
        dq_ref[:, 128 * j:128 * (j + 1)] = (blk * QK_SCALE_64).astype(BF16)
    dkv_ref[:, 0:128] = rot16(z_ref[:, _C_DK:_C_DK + 128])
    dkv_ref[:, 128:256] = z_ref[:, _C_DV:_C_DV + 128]


def in_proj_post(z, tab, gq, gkv, wuq, wk, tm):
    n, zc = z.shape
    row = lambda k: pl.BlockSpec((tm, k), lambda i: (i, 0))
    full = lambda a: pl.BlockSpec(a.shape, lambda i: (0,) * a.ndim)
    out_shapes = (
        jax.ShapeDtypeStruct((n, MLA_HEADS * 256), BF16),
        jax.ShapeDtypeStruct((n, 256), F32),
        jax.ShapeDtypeStruct((n, 1280), F32),
        jax.ShapeDtypeStruct((n, 128), F32),
        jax.ShapeDtypeStruct((n, 1280), BF16),
        jax.ShapeDtypeStruct((n, 256), F32),
    )
    return pl.pallas_call(
        _post_kernel,
        grid=(n // tm,),
        in_specs=[row(zc), row(tab.shape[1]), full(gq), full(gkv), full(wuq), full(wk)],
        out_specs=tuple(row(s.shape[1]) for s in out_shapes),
        out_shape=out_shapes,
        compiler_params=_cparams(("parallel",)),
        name="in_proj_post",
    )(z, tab, gq, gkv, wuq, wk)


def _softmax_update(s, v, m_ref, l_ref, acc_ref):
    m_old = m_ref[...]
    m_new = jnp.maximum(m_old, jnp.max(s, axis=-1, keepdims=True))
    alpha = jnp.exp(m_old - m_new)
    p = jnp.exp(s - m_new)
    l_ref[...] = alpha * l_ref[...] + jnp.sum(p, axis=-1, keepdims=True)
    acc_ref[...] = alpha * acc_ref[...] + jnp.dot(p.astype(BF16), v, preferred_element_type=F32)
    m_ref[...] = m_new


def _mla_prompt_kernel(q_ref, kv_ref, wv_ref, o_ref, m_ref, l_ref, acc_ref, *, tq, tk):
    qt = pl.program_id(1)
    rows = MLA_HEADS * tq
    q = jnp.concatenate([q_ref[:, 256 * h:256 * (h + 1)] for h in range(MLA_HEADS)], axis=0)
    m_ref[...] = jnp.full((rows, 1), -jnp.inf, F32)
    l_ref[...] = jnp.zeros((rows, 1), F32)
    acc_ref[...] = jnp.zeros((rows, 256), F32)
    qpos = qt * tq + lax.broadcasted_iota(jnp.int32, (rows, 1), 0) % tq

    def body(c, carry):
        k = kv_ref[pl.ds(pl.multiple_of(c * tk, tk), tk), :].astype(BF16)
        s = lax.dot_general(q, k, NT_DIMS, preferred_element_type=F32)
        kpos = c * tk + lax.broadcasted_iota(jnp.int32, (1, tk), 1)
        s = jnp.where(kpos <= qpos, s, -jnp.inf)
        _softmax_update(s, k, m_ref, l_ref, acc_ref)
        return carry

    lax.fori_loop(0, (qt * tq + tq + tk - 1) // tk, body, 0)
    o_lat = (acc_ref[...] / l_ref[...]).astype(BF16)
    for h in range(MLA_HEADS):
        o = jnp.dot(o_lat[h * tq:(h + 1) * tq], wv_ref[h], preferred_element_type=F32)
        o_ref[:, 128 * h:128 * (h + 1)] = o.astype(o_ref.dtype)


def mla_prompt(qm, kv, wv, batch, seq, tq=128, tk=512):
    rows = MLA_HEADS * tq
    nq = seq // tq
    return pl.pallas_call(
        functools.partial(_mla_prompt_kernel, tq=tq, tk=tk),
        grid=(batch, nq),
        in_specs=[pl.BlockSpec((tq, MLA_HEADS * 256), lambda b, i: (b * nq + i, 0)),
                  pl.BlockSpec((seq, 256), lambda b, i: (b, 0)),
                  pl.BlockSpec(wv.shape, lambda b, i: (0, 0, 0))],
        out_specs=pl.BlockSpec((tq, MLA_HEADS * 128), lambda b, i: (b * nq + i, 0)),
        out_shape=jax.ShapeDtypeStruct((batch * seq, MLA_HEADS * 128), BF16),
        scratch_shapes=[pltpu.VMEM((rows, 1), F32), pltpu.VMEM((rows, 1), F32),
                        pltpu.VMEM((rows, 256), F32)],
        compiler_params=_cparams(("parallel", "arbitrary")),
        name="mla_prompt",
    )(qm, kv, wv)


def _diff_lambda(lam_ref, lam_init):
    e1 = jnp.exp(jnp.sum(lam_ref[0:1, :] * lam_ref[1:2, :], axis=-1, keepdims=True))
    e2 = jnp.exp(jnp.sum(lam_ref[2:3, :] * lam_ref[3:4, :], axis=-1, keepdims=True))
    return e1 - e2 + lam_init


def _diff_finish(o, g_ref, lam_init):
    y = o * lax.rsqrt(jnp.mean(o * o, axis=-1, keepdims=True) + EPS)
    return y * g_ref[...] * (1.0 - lam_init)


def _diff_prompt_kernel(q_ref, kv_ref, lam_ref, g_ref, o_ref, m_ref, l_ref, acc_ref,
                        *, tq, tk, lam_init):
    qt = pl.program_id(1)
    half = DIFF_HEADS * tq
    rows = 2 * half
    low = lax.broadcasted_iota(jnp.int32, (1, LANES), 1) < 64
    zero = jnp.zeros((), BF16)
    heads = [q_ref[:, 128 * h:128 * (h + 1)] for h in range(DIFF_HEADS)]
    q = jnp.concatenate([jnp.where(low, x, zero) for x in heads]
                        + [jnp.where(low, zero, x) for x in heads], axis=0)
    m_ref[...] = jnp.full((rows, 1), -jnp.inf, F32)
    l_ref[...] = jnp.zeros((rows, 1), F32)
    acc_ref[...] = jnp.zeros((rows, 128), F32)
    qpos = qt * tq + lax.broadcasted_iota(jnp.int32, (rows, 1), 0) % tq

    def body(c, carry):
        start = pl.multiple_of(c * tk, tk)
        k = kv_ref[pl.ds(start, tk), 0:128].astype(BF16)
        v = kv_ref[pl.ds(start, tk), 128:256].astype(BF16)
        s = lax.dot_general(q, k, NT_DIMS, preferred_element_type=F32)
        kpos = c * tk + lax.broadcasted_iota(jnp.int32, (1, tk), 1)
        s = jnp.where(kpos <= qpos, s, -jnp.inf)
        _softmax_update(s, v, m_ref, l_ref, acc_ref)
        return carry

    lax.fori_loop(0, (qt * tq + tq + tk - 1) // tk, body, 0)
    o = acc_ref[...] / l_ref[...]
    o = o[:half] - _diff_lambda(lam_ref, lam_init) * o[half:]
    o = _diff_finish(o, g_ref, lam_init)
    for h in range(DIFF_HEADS):
        o_ref[:, 128 * h:128 * (h + 1)] = o[h * tq:(h + 1) * tq].astype(o_ref.dtype)


def diff_prompt(dq, dkv, lam_vecs, g, lam_init, batch, seq, tq=128, tk=256):
    rows = 2 * DIFF_HEADS * tq
    nq = seq // tq
    return pl.pallas_call(
        functools.partial(_diff_prompt_kernel, tq=tq, tk=tk, lam_init=lam_init),
        grid=(batch, nq),
        in_specs=[pl.BlockSpec((tq, DIFF_HEADS * 128), lambda b, i: (b * nq + i, 0)),
                  pl.BlockSpec((seq, 256), lambda b, i: (b, 0)),
                  pl.BlockSpec(lam_vecs.shape, lambda b, i: (0, 0)),
                  pl.BlockSpec(g.shape, lambda b, i: (0, 0))],
        out_specs=pl.BlockSpec((tq, DIFF_HEADS * 128), lambda b, i: (b * nq + i, 0)),
        out_shape=jax.ShapeDtypeStruct((batch * seq, DIFF_HEADS * 128), BF16),
        scratch_shapes=[pltpu.VMEM((rows, 1), F32), pltpu.VMEM((rows, 1), F32),
                        pltpu.VMEM((rows, 128), F32)],
        compiler_params=_cparams(("parallel", "arbitrary")),
        name="diff_prompt",
    )(dq, dkv, lam_vecs, g)


def _top3_mask(gs, blk, n_valid):
    blk = blk.astype(F32)
    valid = blk < jnp.asarray(n_valid, F32)
    gs = jnp.where(valid, gs, -jnp.inf)
    sentinel = float(gs.shape[-1])
    sel = jnp.zeros(gs.shape, F32)
    for _ in range(MOBA_TOPK):
        mx = jnp.max(gs, axis=-1, keepdims=True)
        idx = jnp.min(jnp.where(gs == mx, blk, sentinel), axis=-1, keepdims=True)
        pick = blk == idx
        sel = jnp.where(pick, 1.0, sel)
        gs = jnp.where(pick, -jnp.inf, gs)
    return jnp.where(valid, sel, 0.0)


def _moba_prompt_kernel(q_ref, k_ref, v_ref, o_ref, m_ref, l_ref, acc_ref, *, tq, nblk):
    qt = pl.program_id(1)
    rows = MOBA_HEADS * tq
    cur = (qt * tq) // MOBA_BLOCK
    qf = q_ref[0].reshape(rows, MOBA_HD)
    qb = (qf * QK_SCALE_64).astype(BF16)

    kmean = jnp.sum(k_ref[...].reshape(nblk, MOBA_BLOCK, MOBA_HD), axis=1) * (1.0 / MOBA_BLOCK)
    gs = lax.dot_general(qf, kmean, NT_DIMS, precision=lax.Precision.HIGHEST,
                         preferred_element_type=F32)
    blk = lax.broadcasted_iota(jnp.int32, (1, nblk), 1)
    sel = _top3_mask(gs, blk, cur)

    start = pl.multiple_of(cur * MOBA_BLOCK, MOBA_BLOCK)
    k = k_ref[pl.ds(start, MOBA_BLOCK), :].astype(BF16)
    v = v_ref[pl.ds(start, MOBA_BLOCK), :].astype(BF16)
    s = lax.dot_general(qb, k, NT_DIMS, preferred_element_type=F32)
    qpos = qt * tq + lax.broadcasted_iota(jnp.int32, (rows, 1), 0) % tq
    kpos = cur * MOBA_BLOCK + lax.broadcasted_iota(jnp.int32, (1, MOBA_BLOCK), 1)
    s = jnp.where(kpos <= qpos, s, -jnp.inf)
    m0 = jnp.max(s, axis=-1, keepdims=True)
    p = jnp.exp(s - m0)
    m_ref[...] = m0
    l_ref[...] = jnp.sum(p, axis=-1, keepdims=True)
    acc_ref[...] = jnp.dot(p.astype(BF16), v, preferred_element_type=F32)

    for n in range(nblk - 1):
        @pl.when(n < cur)
        def _():
            kn = k_ref[n * MOBA_BLOCK:(n + 1) * MOBA_BLOCK, :].astype(BF16)
            vn = v_ref[n * MOBA_BLOCK:(n + 1) * MOBA_BLOCK, :].astype(BF16)
            sn = lax.dot_general(qb, kn, NT_DIMS, preferred_element_type=F32)
            sn = jnp.where(sel[:, n:n + 1] > 0.0, sn, -jnp.inf)
            _softmax_update(sn, vn, m_ref, l_ref, acc_ref)

    o = acc_ref[...] / l_ref[...]
    o_ref[0] = o.reshape(MOBA_HEADS, tq, MOBA_HD).astype(o_ref.dtype)


def moba_prompt(q_hm, k, v, batch, seq, tq=128):
    rows = MOBA_HEADS * tq
    nq = seq // tq
    nblk = seq // MOBA_BLOCK
    return pl.pallas_call(
        functools.partial(_moba_prompt_kernel, tq=tq, nblk=nblk),
        grid=(batch, nq),
        in_specs=[pl.BlockSpec((1, MOBA_HEADS, tq, MOBA_HD), lambda b, i: (b, 0, i, 0)),
                  pl.BlockSpec((seq, MOBA_HD), lambda b, i: (b, 0)),
                  pl.BlockSpec((seq, MOBA_HD), lambda b, i: (b, 0))],
        out_specs=pl.BlockSpec((1, MOBA_HEADS, tq, MOBA_HD), lambda b, i: (b, 0, i, 0)),
        out_shape=jax.ShapeDtypeStruct((batch, MOBA_HEADS, seq, MOBA_HD), BF16),
        scratch_shapes=[pltpu.VMEM((rows, 1), F32), pltpu.VMEM((rows, 1), F32),
                        pltpu.VMEM((rows, MOBA_HD), F32)],
        compiler_params=_cparams(("parallel", "arbitrary")),
        name="moba_prompt",
    )(q_hm, k, v)


def _page_specs(shape_tail, layer, n):
    def spec(i):
        return pl.BlockSpec((None, None) + shape_tail,
                            lambda b, c, pt: (layer, pt[b, c * n + i], 0, 0))
    return [spec(i) for i in range(n)]


def _gather_pages(refs, dtype, axis):
    return jnp.concatenate([r[...].astype(dtype) for r in refs], axis=axis)


def _softmax_update_t(s, vt, m_ref, l_ref, acc_ref):
    m_old = m_ref[...]
    m_new = jnp.maximum(m_old, jnp.max(s, axis=-1, keepdims=True))
    alpha = jnp.exp(m_old - m_new)
    p = jnp.exp(s - m_new)
    l_ref[...] = alpha * l_ref[...] + jnp.sum(p, axis=-1, keepdims=True)
    acc_ref[...] = alpha * acc_ref[...] + lax.dot_general(p.astype(BF16), vt, NT_DIMS,
                                                          preferred_element_type=F32)
    m_ref[...] = m_new


def _mla_decode_kernel(pt_ref, q_ref, ql_ref, qr_ref, knew_ref, wv_ref, *rest, n_pages):
    ckv_refs, kr_refs = rest[:n_pages], rest[n_pages:2 * n_pages]
    o_ref, m_ref, l_ref, acc_ref = rest[2 * n_pages:]
    c = pl.program_id(1)

    @pl.when(c == 0)
    def _():
        m_ref[...] = jnp.full(m_ref.shape, -jnp.inf, F32)
        l_ref[...] = jnp.zeros(l_ref.shape, F32)
        acc_ref[...] = jnp.zeros(acc_ref.shape, F32)

    ckv_t = _gather_pages(ckv_refs, BF16, 1)
    kr_t = _gather_pages(kr_refs, BF16, 1)
    s = (jnp.dot(ql_ref[...], ckv_t, preferred_element_type=F32)
         + jnp.dot(qr_ref[...], kr_t, preferred_element_type=F32))
    _softmax_update_t(s, ckv_t, m_ref, l_ref, acc_ref)

    @pl.when(c == pl.num_programs(1) - 1)
    def _():
        knew = knew_ref[...]
        s_new = jnp.sum(q_ref[...].astype(F32) * knew, axis=-1, keepdims=True)
        m_old = m_ref[...]
        m_new = jnp.maximum(m_old, s_new)
        alpha = jnp.exp(m_old - m_new)
        p_new = jnp.exp(s_new - m_new)
        l = alpha * l_ref[...] + p_new
        acc = alpha * acc_ref[...] + p_new * knew[:, 0:MLA_LORA]
        o_lat = (acc / l).astype(BF16)
        for h in range(MLA_HEADS):
            o = jnp.dot(o_lat, wv_ref[h, 0:MLA_LORA, :], preferred_element_type=F32)
            o_ref[:, 128 * h:128 * (h + 1)] = o[h:h + 1, :].astype(o_ref.dtype)


def mla_decode(page_table, q, knew, wv, cache_ckv, cache_kr, layer):
    nb, n_tab = page_table.shape
    n = PAGES_PER_STEP
    ql, qr = q[:, :, :MLA_LORA], q[:, :, MLA_LORA:]
    hp = q.shape[1]
    per_b = lambda shape: pl.BlockSpec((None,) + shape, lambda b, c, pt: (b, 0, 0))
    grid_spec = pltpu.PrefetchScalarGridSpec(
        num_scalar_prefetch=1,
        grid=(nb, n_tab // n),
        in_specs=[per_b((hp, 256)), per_b((hp, MLA_LORA)), per_b((hp, MLA_ROPE)), per_b((1, 256)),
                  pl.BlockSpec(wv.shape, lambda b, c, pt: (0, 0, 0))]
                 + _page_specs((MLA_LORA, PAGE), layer, n) + _page_specs((MLA_ROPE, PAGE), layer, n),
        out_specs=per_b((1, MLA_HEADS * 128)),
        scratch_shapes=[pltpu.VMEM((hp, 1), F32), pltpu.VMEM((hp, 1), F32),
                        pltpu.VMEM((hp, MLA_LORA), F32)],
    )
    return pl.pallas_call(
        functools.partial(_mla_decode_kernel, n_pages=n),
        grid_spec=grid_spec,
        out_shape=jax.ShapeDtypeStruct((nb, 1, MLA_HEADS * 128), BF16),
        compiler_params=_cparams(("parallel", "arbitrary")),
        name="mla_decode",
    )(page_table, q, ql, qr, knew, wv, *([cache_ckv] * n), *([cache_kr] * n))


def _diff_decode_kernel(pt_ref, q_ref, knew_ref, vnew_ref, lam_ref, g_ref, *rest,
                        n_pages, lam_init):
    k_refs, v_refs = rest[:n_pages], rest[n_pages:2 * n_pages]
    o_ref, m_ref, l_ref, acc_ref = rest[2 * n_pages:]
    c = pl.program_id(1)
    hp = q_ref.shape[0]
    low = lax.broadcasted_iota(jnp.int32, (1, LANES), 1) < 64
    zero = jnp.zeros((), BF16)
    q = jnp.concatenate([jnp.where(low, q_ref[...], zero), jnp.where(low, zero, q_ref[...])],
                        axis=0)

    @pl.when(c == 0)
    def _():
        m_ref[...] = jnp.full(m_ref.shape, -jnp.inf, F32)
        l_ref[...] = jnp.zeros(l_ref.shape, F32)
        acc_ref[...] = jnp.zeros(acc_ref.shape, F32)

    k_t = _gather_pages(k_refs, BF16, 1)
    v = _gather_pages(v_refs, BF16, 0)
    s = jnp.dot(q, k_t, preferred_element_type=F32)
    _softmax_update(s, v, m_ref, l_ref, acc_ref)

    @pl.when(c == pl.num_programs(1) - 1)
    def _():
        s_new = jnp.sum(q.astype(F32) * knew_ref[...], axis=-1, keepdims=True)
        m_old = m_ref[...]
        m_new = jnp.maximum(m_old, s_new)
        alpha = jnp.exp(m_old - m_new)
        p_new = jnp.exp(s_new - m_new)
        l = alpha * l_ref[...] + p_new
        o = (alpha * acc_ref[...] + p_new * vnew_ref[...]) / l
        o = o[:hp] - _diff_lambda(lam_ref, lam_init) * o[hp:]
        o_ref[...] = _diff_finish(o, g_ref, lam_init).astype(o_ref.dtype)


def diff_decode(page_table, q, knew, vnew, lam_vecs, g, lam_init, cache_k, cache_v, layer):
    nb, n_tab = page_table.shape
    n = PAGES_PER_STEP
    hp = q.shape[1]
    per_b = lambda shape: pl.BlockSpec((None,) + shape, lambda b, c, pt: (b, 0, 0))
    grid_spec = pltpu.PrefetchScalarGridSpec(
        num_scalar_prefetch=1,
        grid=(nb, n_tab // n),
        in_specs=[per_b((hp, 128)), per_b((1, 128)), per_b((1, 128)),
                  pl.BlockSpec(lam_vecs.shape, lambda b, c, pt: (0, 0)),
                  pl.BlockSpec(g.shape, lambda b, c, pt: (0, 0))]
                 + _page_specs((PAGE, 128), layer, n) + _page_specs((PAGE, 128), layer, n),
        out_specs=per_b((hp, 128)),
        scratch_shapes=[pltpu.VMEM((2 * hp, 1), F32), pltpu.VMEM((2 * hp, 1), F32),
                        pltpu.VMEM((2 * hp, 128), F32)],
    )
    return pl.pallas_call(
        functools.partial(_diff_decode_kernel, n_pages=n, lam_init=lam_init),
        grid_spec=grid_spec,
        out_shape=jax.ShapeDtypeStruct((nb, hp, 128), BF16),
        compiler_params=_cparams(("parallel", "arbitrary")),
        name="diff_decode",
    )(page_table, q, knew, vnew, lam_vecs, g, *([cache_k] * n), *([cache_v] * n))


def _moba_decode_kernel(pt_ref, q_ref, knew_ref, vnew_ref, *rest, n_pages, n_past_blocks):
    k_refs, v_refs = rest[:n_pages], rest[n_pages:2 * n_pages]
    o_ref, km_ref, m_ref, l_ref, oblk_ref = rest[2 * n_pages:]
    c = pl.program_id(1)
    hp = q_ref.shape[0]
    bps = n_pages * PAGE // MOBA_BLOCK
    qf = q_ref[...]
    qb = (qf * QK_SCALE_64).astype(BF16)
    lane = lax.broadcasted_iota(jnp.int32, (1, LANES), 1)

    @pl.when(c == 0)
    def _():
        km_ref[...] = jnp.zeros(km_ref.shape, F32)
        m_ref[...] = jnp.zeros(m_ref.shape, F32)
        l_ref[...] = jnp.zeros(l_ref.shape, F32)

    kf_t = _gather_pages(k_refs, F32, 1)
    v_t = _gather_pages(v_refs, BF16, 1)
    s = jnp.dot(qb, kf_t.astype(BF16), preferred_element_type=F32)
    for j in range(bps):
        cols = slice(j * MOBA_BLOCK, (j + 1) * MOBA_BLOCK)
        here = lane == c * bps + j
        kmean = jnp.sum(kf_t[:, cols], axis=-1, keepdims=True) * (1.0 / MOBA_BLOCK)
        km_ref[...] = jnp.where(here, kmean, km_ref[...])
        sj = s[:, cols]
        mj = jnp.max(sj, axis=-1, keepdims=True)
        pj = jnp.exp(sj - mj)
        lj = jnp.sum(pj, axis=-1, keepdims=True)
        m_ref[...] = jnp.where(here, mj, m_ref[...])
        l_ref[...] = jnp.where(here, lj, l_ref[...])
        oblk_ref[c * bps + j] = lax.dot_general(pj.astype(BF16), v_t[:, cols], NT_DIMS,
                                                preferred_element_type=F32)

    @pl.when(c == pl.num_programs(1) - 1)
    def _():
        gs = jnp.dot(qf, km_ref[...], precision=lax.Precision.HIGHEST,
                     preferred_element_type=F32)
        sel = _top3_mask(gs, lane, n_past_blocks)
        s_new = jnp.sum(qf * QK_SCALE_64 * knew_ref[...], axis=-1, keepdims=True)
        mb = m_ref[...]
        m_all = jnp.maximum(jnp.max(jnp.where(sel > 0.0, mb, -jnp.inf), axis=-1, keepdims=True),
                            s_new)
        coef = jnp.where(sel > 0.0, jnp.exp(mb - m_all), 0.0)
        e_new = jnp.exp(s_new - m_all)
        denom = jnp.sum(coef * l_ref[...], axis=-1, keepdims=True) + e_new
        o = e_new * vnew_ref[...]
        for n in range(n_past_blocks):
            o = o + coef[:, n:n + 1] * oblk_ref[n]
        o_ref[...] = (o / denom).astype(o_ref.dtype)


def moba_decode(page_table, q, knew, vnew, cache_k, cache_v, layer):
    nb, n_tab = page_table.shape
    n = PAGES_PER_STEP
    hp = q.shape[1]
    nblk = n_tab * PAGE // MOBA_BLOCK
    assert nblk <= LANES
    per_b = lambda shape: pl.BlockSpec((None,) + shape, lambda b, c, pt: (b, 0, 0))
    grid_spec = pltpu.PrefetchScalarGridSpec(
        num_scalar_prefetch=1,
        grid=(nb, n_tab // n),
        in_specs=[per_b((hp, MOBA_HD)), per_b((1, MOBA_HD)), per_b((1, MOBA_HD))]
                 + _page_specs((MOBA_HD, PAGE), layer, n) + _page_specs((MOBA_HD, PAGE), layer, n),
        out_specs=per_b((hp, MOBA_HD)),
        scratch_shapes=[pltpu.VMEM((MOBA_HD, LANES), F32), pltpu.VMEM((hp, LANES), F32),
                        pltpu.VMEM((hp, LANES), F32), pltpu.VMEM((nblk, hp, MOBA_HD), F32)],
    )
    return pl.pallas_call(
        functools.partial(_moba_decode_kernel, n_pages=n, n_past_blocks=nblk),
        grid_spec=grid_spec,
        out_shape=jax.ShapeDtypeStruct((nb, hp, MOBA_HD), BF16),
        compiler_params=_cparams(("parallel", "arbitrary")),
        name="moba_decode",
    )(page_table, q, knew, vnew, *([cache_k] * n), *([cache_v] * n))


def _rope_tables(pos):
    def blocks(rot_dim):
        half = rot_dim // 2
        inv = ROPE_THETA ** (-jnp.arange(half, dtype=F32) / half)
        ang = pos.astype(F32)[:, None] * inv[None, :]
        cos, sin = jnp.cos(ang), jnp.sin(ang)
        n = pos.shape[0]
        rest = 64 - rot_dim
        one, zero = jnp.ones((n, rest), F32), jnp.zeros((n, rest + half), F32)
        c = jnp.concatenate([cos, cos, one], axis=1)
        sa = jnp.concatenate([-sin, zero], axis=1)
        sb = jnp.concatenate([zero[:, :half], sin, zero[:, :rest]], axis=1)
        return [jnp.tile(t, (1, 2)) for t in (c, sa, sb)]
    return jnp.concatenate(blocks(16) + blocks(64), axis=1)


def _row_tile(n, target):
    return max(t for t in range(16, min(n, target) + 1, 16) if n % t == 0)


def _pad_axis(a, axis, size):
    pad = [(0, 0)] * a.ndim
    pad[axis] = (0, size - a.shape[axis])
    return jnp.pad(a, pad)


def kernel(x_prompt, x_sample, cache_mla_ckv, cache_mla_krope, cache_moba_k, cache_moba_v,
           cache_diff_k, cache_diff_v, page_table, ffn1_norm_g, ffn1_w_gate, ffn1_w_up, ffn1_w_down,
           mix_norm_g, w_in, mla_q_norm_g, mla_w_uq, mla_kv_norm_g, mla_w_ukv,
           diff_lambda_q1, diff_lambda_k1, diff_lambda_q2, diff_lambda_k2, diff_subln_g, w_out,
           ffn2_norm_g, ffn2_w_gate, ffn2_w_up, ffn2_w_down, final_norm_g):
    batch, seq, d_model = x_prompt.shape
    dec_b = x_sample.shape[0]
    depth = w_in.shape[0]
    n_prompt = batch * seq
    n_rows = n_prompt + dec_b
    past_len = page_table.shape[1] * PAGE
    n_pool = cache_mla_ckv.shape[1]

    tm_big, tm_mid, tm_small = (_row_tile(n_rows, t) for t in (1040, 520, 320))

    x = jnp.concatenate([x_prompt.reshape(n_prompt, d_model), x_sample.reshape(dec_b, d_model)], axis=0)
    pos = jnp.concatenate([jnp.tile(jnp.arange(seq), batch), jnp.full((dec_b,), past_len)])
    tab = _rope_tables(pos)

    ck_mla = jnp.transpose(cache_mla_ckv, (0, 1, 3, 2))
    ck_kr = jnp.transpose(cache_mla_krope, (0, 1, 3, 2))
    ck_mk = jnp.transpose(cache_moba_k, (0, 1, 3, 4, 2)).reshape(depth, n_pool, MOBA_HD, PAGE)
    ck_mv = jnp.transpose(cache_moba_v, (0, 1, 3, 4, 2)).reshape(depth, n_pool, MOBA_HD, PAGE)
    ck_dk = jnp.transpose(cache_diff_k, (0, 1, 3, 4, 5, 2)).reshape(depth, n_pool, 2 * DIFF_D, PAGE)
    ck_dv = cache_diff_v.reshape(depth, n_pool, PAGE, 2 * DIFF_D)

    def ffn(x, g, wg, wu, wd):
        xn = rmsnorm(x, g, BF16, tm_small)
        h = ffn_gateup(xn, wg.astype(BF16), wu.astype(BF16), tm_big, 512)
        return ffn_down(h, wd.astype(BF16), x, tm_mid, 512)

    rows_out = [[] for _ in range(6)]
    for l in range(depth):
        x = ffn(x, ffn1_norm_g[l], ffn1_w_gate[l], ffn1_w_up[l], ffn1_w_down[l])

        hn = rmsnorm(x, mix_norm_g[l], BF16, tm_small)
        w_in_p = _pad_axis(w_in[l], 1, 4096).astype(BF16)
        z = matmul_f32out(hn, w_in_p, tm_big, 512)

        wuq = mla_w_uq[l].reshape(-1, MLA_HEADS, MLA_NOPE + MLA_ROPE)
        wuq_rope = jnp.pad(wuq[:, :, MLA_NOPE:], ((0, 0), (0, 0), (64, 0)))
        wuq_p = jnp.concatenate([wuq[:, :, :MLA_NOPE].reshape(-1, MLA_HEADS * 128),
                                 wuq_rope.reshape(-1, MLA_HEADS * 128)], axis=1).astype(BF16)
        wukv = mla_w_ukv[l].reshape(MLA_LORA, MLA_HEADS, MLA_NOPE + 128)
        wk = _pad_axis(jnp.transpose(wukv[:, :, :MLA_NOPE], (1, 2, 0)), 2, 256).astype(BF16)
        wv = _pad_axis(jnp.transpose(wukv[:, :, MLA_NOPE:], (1, 0, 2)), 1, 256).astype(BF16)
        gq = mla_q_norm_g[l].reshape(1, -1)
        gkv = _pad_axis(mla_kv_norm_g[l].reshape(1, -1), 1, 256)

        qm, kv, mq, mkv, dq, dkv = in_proj_post(z, tab, gq, gkv, wuq_p, wk, tm_small)
        r_mk, r_mv = mkv[:, :MOBA_HD], mkv[:, MOBA_HD:]
        lam_vecs = jnp.stack([diff_lambda_q1[l], diff_lambda_k1[l], diff_lambda_q2[l], diff_lambda_k2[l]])
        lam_init = 0.8 - 0.6 * math.exp(-0.3 * l)
        g_sub = diff_subln_g[l].reshape(1, -1)

        oa_p = mla_prompt(qm, kv, wv, batch, seq)
        oc_p = diff_prompt(dq, dkv, lam_vecs, g_sub, lam_init, batch, seq)
        mq_hm = jnp.transpose(mq[:n_prompt].reshape(batch, seq, MOBA_HEADS, MOBA_HD), (0, 2, 1, 3))
        ob_p = moba_prompt(mq_hm, r_mk, r_mv, batch, seq)
        ob_p = jnp.transpose(ob_p, (0, 2, 1, 3)).reshape(n_prompt, MOBA_HEADS * MOBA_HD)

        qm_d = _pad_axis(qm[n_prompt:].reshape(dec_b, MLA_HEADS, 256), 1, 16)
        oa_d = mla_decode(page_table, qm_d, kv[n_prompt:].reshape(dec_b, 1, 256), wv,
                          ck_mla, ck_kr, l).reshape(dec_b, MLA_HEADS * 128)
        dq_d = _pad_axis(dq[n_prompt:].reshape(dec_b, DIFF_HEADS, 128), 1, 16)
        oc_d = diff_decode(page_table, dq_d, dkv[n_prompt:, :128].reshape(dec_b, 1, 128),
                           dkv[n_prompt:, 128:].reshape(dec_b, 1, 128), lam_vecs, g_sub, lam_init,
                           ck_dk, ck_dv, l)
        oc_d = oc_d[:, :DIFF_HEADS].reshape(dec_b, DIFF_HEADS * 128)
        mq_d = _pad_axis(mq[n_prompt:].reshape(dec_b, MOBA_HEADS, MOBA_HD), 1, 32)
        ob_d = moba_decode(page_table, mq_d, r_mk[n_prompt:].reshape(dec_b, 1, MOBA_HD),
                           r_mv[n_prompt:].reshape(dec_b, 1, MOBA_HD), ck_mk, ck_mv, l)
        ob_d = ob_d[:, :MOBA_HEADS].reshape(dec_b, MOBA_HEADS * MOBA_HD)

        oa = jnp.concatenate([oa_p, oa_d], axis=0)
        ob = jnp.concatenate([ob_p, ob_d], axis=0)
        oc = jnp.concatenate([oc_p, oc_d], axis=0)
        wo = w_out[l].astype(BF16)
        n_a, n_b = MLA_HEADS * 128, MOBA_HEADS * MOBA_HD
        x = out_proj(oa, ob, oc, wo[:n_a], wo[n_a:n_a + n_b], wo[n_a + n_b:], x, tm_big, 512)

        x = ffn(x, ffn2_norm_g[l], ffn2_w_gate[l], ffn2_w_up[l], ffn2_w_down[l])

        for acc, r in zip(rows_out, (kv[:, :MLA_LORA], kv[:, MLA_LORA:], r_mk, r_mv,
                                     dkv[:, :128], dkv[:, 128:])):
            acc.append(r)

    y = rmsnorm(x, final_norm_g, F32, tm_small)
    y_prompt = y[:n_prompt].reshape(batch, seq, d_model)
    y_sample = y[n_prompt:].reshape(dec_b, 1, d_model)

    tails = ((MLA_LORA,), (MLA_ROPE,), (1, MOBA_HD), (1, MOBA_HD), (1, 2, DIFF_D), (1, 2 * DIFF_D))
    stacked = [jnp.stack(a) for a in rows_out]
    prompt_rows = tuple(s[:, :n_prompt].reshape((depth, batch, seq) + t) for s, t in zip(stacked, tails))
    sample_rows = tuple(s[:, n_prompt:].reshape((depth, dec_b, 1) + t) for s, t in zip(stacked, tails))
    return (y_prompt, y_sample) + prompt_rows + sample_rows
```

```python
import functools
import math

import jax
import jax.numpy as jnp
from jax import lax
from jax.experimental import pallas as pl
from jax.experimental.pallas import tpu as pltpu

F32 = jnp.float32
BF16 = jnp.bfloat16

EPS = 1e-6
ROPE_THETA = 500000.0
FFN_RES_WEIGHT = 0.5
MLA_HEADS = 12
MLA_NOPE = 128
MLA_ROPE = 64
MLA_LORA = 192
MLA_SCALE = (MLA_NOPE + MLA_ROPE) ** -0.5
MOBA_HEADS = 20
MOBA_HD = 64
MOBA_BLOCK = 256
MOBA_TOPK = 3
DIFF_HEADS = 10
DIFF_D = 64
QK_SCALE_64 = 0.125
PAGE = 128
PAGES_PER_STEP = 16
Q_TILE = 128
K_TILE = 512
LANES = 128
VMEM_LIMIT = 56 * 1024 * 1024

NT_DIMS = (((1,), (1,)), ((), ()))


def _cparams(sem):
    return pltpu.CompilerParams(dimension_semantics=sem, vmem_limit_bytes=VMEM_LIMIT)


def _rmsnorm_kernel(x_ref, g_ref, o_ref):
    x = x_ref[...]
    y = x * lax.rsqrt(jnp.mean(x * x, axis=-1, keepdims=True) + EPS)
    o_ref[...] = (y * g_ref[...]).astype(o_ref.dtype)


def rmsnorm(x, g, out_dtype, tm):
    n, d = x.shape
    return pl.pallas_call(
        _rmsnorm_kernel,
        grid=(n // tm,),
        in_specs=[pl.BlockSpec((tm, d), lambda i: (i, 0)),
                  pl.BlockSpec((1, d), lambda i: (0, 0))],
        out_specs=pl.BlockSpec((tm, d), lambda i: (i, 0)),
        out_shape=jax.ShapeDtypeStruct((n, d), out_dtype),
        compiler_params=_cparams(("parallel",)),
        name="rmsnorm",
    )(x, g.reshape(1, d))


def _gateup_kernel(x_ref, wg_ref, wu_ref, o_ref):
    x = x_ref[...]
    g = jnp.dot(x, wg_ref[...], preferred_element_type=F32)
    u = jnp.dot(x, wu_ref[...], preferred_element_type=F32)
    o_ref[...] = (g * jax.nn.sigmoid(g) * u).astype(o_ref.dtype)


def ffn_gateup(xn, wg, wu, tm, tn):
    n, d = xn.shape
    f = wg.shape[1]
    return pl.pallas_call(
        _gateup_kernel,
        grid=(n // tm, f // tn),
        in_specs=[pl.BlockSpec((tm, d), lambda i, j: (i, 0)),
                  pl.BlockSpec((d, tn), lambda i, j: (0, j)),
                  pl.BlockSpec((d, tn), lambda i, j: (0, j))],
        out_specs=pl.BlockSpec((tm, tn), lambda i, j: (i, j)),
        out_shape=jax.ShapeDtypeStruct((n, f), BF16),
        compiler_params=_cparams(("parallel", "arbitrary")),
        name="ffn_gateup",
    )(xn, wg, wu)


def _down_kernel(h_ref, w_ref, x_ref, o_ref):
    y = jnp.dot(h_ref[...], w_ref[...], preferred_element_type=F32)
    o_ref[...] = x_ref[...] + FFN_RES_WEIGHT * y


def ffn_down(h, wd, x, tm, tn):
    n, f = h.shape
    d = wd.shape[1]
    return pl.pallas_call(
        _down_kernel,
        grid=(n // tm, d // tn),
        in_specs=[pl.BlockSpec((tm, f), lambda i, j: (i, 0)),
                  pl.BlockSpec((f, tn), lambda i, j: (0, j)),
                  pl.BlockSpec((tm, tn), lambda i, j: (i, j))],
        out_specs=pl.BlockSpec((tm, tn), lambda i, j: (i, j)),
        out_shape=jax.ShapeDtypeStruct((n, d), F32),
        compiler_params=_cparams(("parallel", "arbitrary")),
        name="ffn_down",
    )(h, wd, x)


def _matmul_kernel(x_ref, w_ref, o_ref):
    o_ref[...] = jnp.dot(x_ref[...], w_ref[...], preferred_element_type=F32)


def matmul_f32out(x, w, tm, tn):
    n, d = x.shape
    f = w.shape[1]
    return pl.pallas_call(
        _matmul_kernel,
        grid=(n // tm, f // tn),
        in_specs=[pl.BlockSpec((tm, d), lambda i, j: (i, 0)),
                  pl.BlockSpec((d, tn), lambda i, j: (0, j))],
        out_specs=pl.BlockSpec((tm, tn), lambda i, j: (i, j)),
        out_shape=jax.ShapeDtypeStruct((n, f), F32),
        compiler_params=_cparams(("parallel", "arbitrary")),
        name="in_proj",
    )(x, w)


def _outproj_kernel(a_ref, b_ref, c_ref, wa_ref, wb_ref, wc_ref, x_ref, o_ref):
    y = jnp.dot(a_ref[...], wa_ref[...], preferred_element_type=F32)
    y = y + jnp.dot(b_ref[...], wb_ref[...], preferred_element_type=F32)
    y = y + jnp.dot(c_ref[...], wc_ref[...], preferred_element_type=F32)
    o_ref[...] = x_ref[...] + y


def out_proj(a, b, c, wa, wb, wc, x, tm, tn):
    n = a.shape[0]
    d = wa.shape[1]
    act = lambda k: pl.BlockSpec((tm, k), lambda i, j: (i, 0))
    wsp = lambda k: pl.BlockSpec((k, tn), lambda i, j: (0, j))
    return pl.pallas_call(
        _outproj_kernel,
        grid=(n // tm, d // tn),
        in_specs=[act(a.shape[1]), act(b.shape[1]), act(c.shape[1]),
                  wsp(wa.shape[0]), wsp(wb.shape[0]), wsp(wc.shape[0]),
                  pl.BlockSpec((tm, tn), lambda i, j: (i, j))],
        out_specs=pl.BlockSpec((tm, tn), lambda i, j: (i, j)),
        out_shape=jax.ShapeDtypeStruct((n, d), F32),
        compiler_params=_cparams(("parallel", "arbitrary")),
        name="out_proj",
    )(a, b, c, wa, wb, wc, x)


_C_CQ, _C_KV, _C_MQ, _C_MKV, _C_DQ, _C_DK, _C_DV = 0, 768, 1024, 2304, 2432, 3712, 3840
_Q_ROPE0 = MLA_HEADS * MLA_NOPE


def _rot(blk, c, sa, sb, half):
    return (blk * c + pltpu.roll(blk, LANES - half, 1) * sa + pltpu.roll(blk, half, 1) * sb)


def _post_kernel(z_ref, tab_ref, gq_ref, gkv_ref, wuq_ref, wk_ref,
                 qm_ref, kv_ref, mq_ref, mkv_ref, dq_ref, dkv_ref):
    c16, a16, b16 = tab_ref[:, 0:128], tab_ref[:, 128:256], tab_ref[:, 256:384]
    c64, a64, b64 = tab_ref[:, 384:512], tab_ref[:, 512:640], tab_ref[:, 640:768]
    rot16 = lambda blk: _rot(blk, c16, a16, b16, 8)
    rot64 = lambda blk: _rot(blk, c64, a64, b64, 32)
    lane = lax.broadcasted_iota(jnp.int32, (1, LANES), 1)
    low = lane < 64

    cq = z_ref[:, _C_CQ:_C_CQ + 768]
    cqn = cq * lax.rsqrt(jnp.mean(cq * cq, axis=-1, keepdims=True) + EPS) * gq_ref[...]
    q = jnp.dot(cqn.astype(BF16), wuq_ref[...], preferred_element_type=F32)
    for h in range(MLA_HEADS):
        ql = jnp.dot(q[:, 128 * h:128 * (h + 1)].astype(BF16), wk_ref[h],
                     preferred_element_type=F32)
        rb = rot64(q[:, _Q_ROPE0 + 128 * h:_Q_ROPE0 + 128 * (h + 1)])
        qm_ref[:, 256 * h:256 * h + 128] = (ql[:, :128] * MLA_SCALE).astype(BF16)
        qm_ref[:, 256 * h + 128:256 * h + 256] = ((ql[:, 128:] + rb) * MLA_SCALE).astype(BF16)

    k0 = z_ref[:, _C_KV:_C_KV + 128]
    k1 = z_ref[:, _C_KV + 128:_C_KV + 256]
    ss = (jnp.sum(k0 * k0, axis=-1, keepdims=True)
          + jnp.sum(jnp.where(low, k1 * k1, 0.0), axis=-1, keepdims=True))
    r = lax.rsqrt(ss * (1.0 / MLA_LORA) + EPS)
    kv_ref[:, 0:128] = k0 * r * gkv_ref[:, 0:128]
    kv_ref[:, 128:256] = jnp.where(low, k1 * r * gkv_ref[:, 128:256], rot64(k1))

    for j in range(MOBA_HEADS // 2):
        mq_ref[:, 128 * j:128 * (j + 1)] = rot16(z_ref[:, _C_MQ + 128 * j:_C_MQ + 128 * (j + 1)])
    mkv = z_ref[:, _C_MKV:_C_MKV + 128]
    mkv_ref[...] = jnp.where(low, rot16(mkv), mkv)

    for j in range(DIFF_HEADS):
        blk = rot16(z_ref[:, _C_DQ + 128 * j:_C_DQ + 128 * (j + 1)])
        dq_ref[:, 128 * j:128 * (j + 1)] = (blk * QK_SCALE_64).astype(BF16)
    dkv_ref[:, 0:128] = rot16(z_ref[:, _C_DK:_C_DK + 128])
    dkv_ref[:, 128:256] = z_ref[:, _C_DV:_C_DV + 128]


def in_proj_post(z, tab, gq, gkv, wuq, wk, tm):
    n, zc = z.shape
    row = lambda k: pl.BlockSpec((tm, k), lambda i: (i, 0))
    full = lambda a: pl.BlockSpec(a.shape, lambda i: (0,) * a.ndim)
    out_shapes = (
        jax.ShapeDtypeStruct((n, MLA_HEADS * 256), BF16),
        jax.ShapeDtypeStruct((n, 256), F32),
        jax.ShapeDtypeStruct((n, 1280), F32),
        jax.ShapeDtypeStruct((n, 128), F32),
        jax.ShapeDtypeStruct((n, 1280), BF16),
        jax.ShapeDtypeStruct((n, 256), F32),
    )
    return pl.pallas_call(
        _post_kernel,
        grid=(n // tm,),
        in_specs=[row(zc), row(tab.shape[1]), full(gq), full(gkv), full(wuq), full(wk)],
        out_specs=tuple(row(s.shape[1]) for s in out_shapes),
        out_shape=out_shapes,
        compiler_params=_cparams(("parallel",)),
        name="in_proj_post",
    )(z, tab, gq, gkv, wuq, wk)


def _softmax_update_cols(s_t, v_t, m_ref, l_ref, acc_ref):
    m_old = m_ref[...]
    m_new = jnp.maximum(m_old, jnp.max(s_t, axis=0, keepdims=True))
    alpha = jnp.exp(m_old - m_new)
    p_t = jnp.exp(s_t - m_new)
    l_ref[...] = alpha * l_ref[...] + jnp.sum(p_t, axis=0, keepdims=True)
    acc_ref[...] = alpha * acc_ref[...] + jnp.dot(v_t, p_t.astype(BF16), preferred_element_type=F32)
    m_ref[...] = m_new


def _causal_cols(s_t, key0, q0, tq):
    kpos = key0 + lax.broadcasted_iota(jnp.int32, (s_t.shape[0], 1), 0)
    qpos = q0 + lax.broadcasted_iota(jnp.int32, (1, s_t.shape[1]), 1) % tq
    return jnp.where(kpos <= qpos, s_t, -jnp.inf)


def _init_stats(m_ref, l_ref, acc_ref):
    m_ref[...] = jnp.full(m_ref.shape, -jnp.inf, F32)
    l_ref[...] = jnp.zeros(l_ref.shape, F32)
    acc_ref[...] = jnp.zeros(acc_ref.shape, F32)


def _mla_prompt_kernel(q_ref, k_ref, kt_ref, wvt_ref, o_ref, m_ref, l_ref, acc_ref, *, tq, tk):
    qt = pl.program_id(1)
    q_t = q_ref[...]
    _init_stats(m_ref, l_ref, acc_ref)

    def body(c, carry):
        s_t = jnp.dot(k_ref[c], q_t, preferred_element_type=F32)
        s_t = _causal_cols(s_t, c * tk, qt * tq, tq)
        _softmax_update_cols(s_t, kt_ref[c], m_ref, l_ref, acc_ref)
        return carry

    lax.fori_loop(0, (qt * tq + tq + tk - 1) // tk, body, 0)
    o_lat = (acc_ref[...] / l_ref[...]).astype(BF16)
    for h in range(MLA_HEADS):
        o_ref[128 * h:128 * (h + 1), :] = jnp.dot(
            wvt_ref[h], o_lat[:, h * tq:(h + 1) * tq], preferred_element_type=F32).astype(o_ref.dtype)


def mla_prompt(q_t, k, k_t, wv_t, tq, tk):
    batch, nq = q_t.shape[:2]
    cols = MLA_HEADS * tq
    return pl.pallas_call(
        functools.partial(_mla_prompt_kernel, tq=tq, tk=tk),
        grid=(batch, nq),
        in_specs=[pl.BlockSpec((None, None, 256, cols), lambda b, i: (b, i, 0, 0)),
                  pl.BlockSpec((None,) + k.shape[1:], lambda b, i: (b, 0, 0, 0)),
                  pl.BlockSpec((None,) + k_t.shape[1:], lambda b, i: (b, 0, 0, 0)),
                  pl.BlockSpec(wv_t.shape, lambda b, i: (0, 0, 0))],
        out_specs=pl.BlockSpec((None, None, MLA_HEADS * 128, tq), lambda b, i: (b, i, 0, 0)),
        out_shape=jax.ShapeDtypeStruct((batch, nq, MLA_HEADS * 128, tq), BF16),
        scratch_shapes=[pltpu.VMEM((1, cols), F32), pltpu.VMEM((1, cols), F32),
                        pltpu.VMEM((256, cols), F32)],
        compiler_params=_cparams(("parallel", "arbitrary")),
        name="mla_prompt",
    )(q_t, k, k_t, wv_t)


def _diff_lambda(lam_ref, lam_init):
    e1 = jnp.exp(jnp.sum(lam_ref[0:1, :] * lam_ref[1:2, :], axis=-1, keepdims=True))
    e2 = jnp.exp(jnp.sum(lam_ref[2:3, :] * lam_ref[3:4, :], axis=-1, keepdims=True))
    return e1 - e2 + lam_init


def _diff_prompt_kernel(q_ref, k_ref, vt_ref, lam_ref, g_ref, o_ref, m_ref, l_ref, acc_ref,
                        *, tq, tk, lam_init):
    qt = pl.program_id(1)
    half = DIFF_HEADS * tq
    first = lax.broadcasted_iota(jnp.int32, (2 * DIFF_D, 1), 0) < DIFF_D
    zero = jnp.zeros((), BF16)
    q = q_ref[...]
    q_t = jnp.concatenate([jnp.where(first, q, zero), jnp.where(first, zero, q)], axis=1)
    _init_stats(m_ref, l_ref, acc_ref)

    def body(c, carry):
        s_t = jnp.dot(k_ref[c], q_t, preferred_element_type=F32)
        s_t = _causal_cols(s_t, c * tk, qt * tq, tq)
        _softmax_update_cols(s_t, vt_ref[c], m_ref, l_ref, acc_ref)
        return carry

    lax.fori_loop(0, (qt * tq + tq + tk - 1) // tk, body, 0)
    o = acc_ref[...] / l_ref[...]
    o = o[:, :half] - _diff_lambda(lam_ref, lam_init) * o[:, half:]
    y = o * lax.rsqrt(jnp.mean(o * o, axis=0, keepdims=True) + EPS)
    o_ref[...] = (y * g_ref[...] * (1.0 - lam_init)).astype(o_ref.dtype)


def diff_prompt(q_t, k, v_t, lam_vecs, g_col, lam_init, tq, tk):
    batch, nq = q_t.shape[:2]
    cols = DIFF_HEADS * tq
    return pl.pallas_call(
        functools.partial(_diff_prompt_kernel, tq=tq, tk=tk, lam_init=lam_init),
        grid=(batch, nq),
        in_specs=[pl.BlockSpec((None, None, 128, cols), lambda b, i: (b, i, 0, 0)),
                  pl.BlockSpec((None,) + k.shape[1:], lambda b, i: (b, 0, 0, 0)),
                  pl.BlockSpec((None,) + v_t.shape[1:], lambda b, i: (b, 0, 0, 0)),
                  pl.BlockSpec(lam_vecs.shape, lambda b, i: (0, 0)),
                  pl.BlockSpec(g_col.shape, lambda b, i: (0, 0))],
        out_specs=pl.BlockSpec((None, None, 128, cols), lambda b, i: (b, i, 0, 0)),
        out_shape=jax.ShapeDtypeStruct((batch, nq, 128, cols), BF16),
        scratch_shapes=[pltpu.VMEM((1, 2 * cols), F32), pltpu.VMEM((1, 2 * cols), F32),
                        pltpu.VMEM((128, 2 * cols), F32)],
        compiler_params=_cparams(("parallel", "arbitrary")),
        name="diff_prompt",
    )(q_t, k, v_t, lam_vecs, g_col)


def _top3_mask(gs, blk, n_valid, axis):
    blk = blk.astype(F32)
    valid = blk < jnp.asarray(n_valid, F32)
    gs = jnp.where(valid, gs, -jnp.inf)
    sentinel = float(gs.shape[axis])
    sel = jnp.zeros(gs.shape, F32)
    for _ in range(MOBA_TOPK):
        mx = jnp.max(gs, axis=axis, keepdims=True)
        idx = jnp.min(jnp.where(gs == mx, blk, sentinel), axis=axis, keepdims=True)
        pick = blk == idx
        sel = jnp.where(pick, 1.0, sel)
        gs = jnp.where(pick, -jnp.inf, gs)
    return jnp.where(valid, sel, 0.0)


def _moba_prompt_kernel(q_ref, k_ref, vt_ref, o_ref, m_ref, l_ref, acc_ref, *, tq):
    qt = pl.program_id(1)
    nblk = k_ref.shape[0]
    cur = (qt * tq) // MOBA_BLOCK
    qf = q_ref[...]
    qb = (qf * QK_SCALE_64).astype(BF16)

    kmean = jnp.sum(k_ref[...], axis=1) * (1.0 / MOBA_BLOCK)
    gs = jnp.dot(kmean, qf, precision=lax.Precision.HIGHEST, preferred_element_type=F32)
    blk = lax.broadcasted_iota(jnp.int32, (nblk, 1), 0)
    sel = _top3_mask(gs, blk, cur, 0)

    s_t = jnp.dot(k_ref[cur].astype(BF16), qb, preferred_element_type=F32)
    s_t = _causal_cols(s_t, cur * MOBA_BLOCK, qt * tq, tq)
    m0 = jnp.max(s_t, axis=0, keepdims=True)
    p_t = jnp.exp(s_t - m0)
    m_ref[...] = m0
    l_ref[...] = jnp.sum(p_t, axis=0, keepdims=True)
    acc_ref[...] = jnp.dot(vt_ref[cur], p_t.astype(BF16), preferred_element_type=F32)

    for n in range(nblk - 1):
        @pl.when(n < cur)
        def _():
            sn = jnp.dot(k_ref[n].astype(BF16), qb, preferred_element_type=F32)
            sn = jnp.where(sel[n:n + 1, :] > 0.0, sn, -jnp.inf)
            _softmax_update_cols(sn, vt_ref[n], m_ref, l_ref, acc_ref)

    o_ref[...] = (acc_ref[...] / l_ref[...]).astype(o_ref.dtype)


def moba_prompt(q_t, k, v_t, tq):
    batch, nq = q_t.shape[:2]
    cols = MOBA_HEADS * tq
    return pl.pallas_call(
        functools.partial(_moba_prompt_kernel, tq=tq),
        grid=(batch, nq),
        in_specs=[pl.BlockSpec((None, None, MOBA_HD, cols), lambda b, i: (b, i, 0, 0)),
                  pl.BlockSpec((None,) + k.shape[1:], lambda b, i: (b, 0, 0, 0)),
                  pl.BlockSpec((None,) + v_t.shape[1:], lambda b, i: (b, 0, 0, 0))],
        out_specs=pl.BlockSpec((None, None, MOBA_HD, cols), lambda b, i: (b, i, 0, 0)),
        out_shape=jax.ShapeDtypeStruct((batch, nq, MOBA_HD, cols), BF16),
        scratch_shapes=[pltpu.VMEM((1, cols), F32), pltpu.VMEM((1, cols), F32),
                        pltpu.VMEM((MOBA_HD, cols), F32)],
        compiler_params=_cparams(("parallel", "arbitrary")),
        name="moba_prompt",
    )(q_t, k, v_t)


def _paged_loop(pt_ref, layer, caches, bufs, sems, n_pages, consume):
    nb, n_tab = pt_ref.shape
    n_groups = n_tab // n_pages
    n_iter = nb * n_groups

    def copies(it, slot):
        b, c = it // n_groups, it % n_groups
        out = []
        for i in range(n_pages):
            page = pt_ref[b, c * n_pages + i]
            for j, (cache, buf) in enumerate(zip(caches, bufs)):
                out.append(pltpu.make_async_copy(cache.at[layer, page], buf.at[slot, i], sems.at[slot, j]))
        return out

    for cp in copies(0, 0):
        cp.start()

    def body(it, carry):
        slot = it % 2

        @pl.when(it + 1 < n_iter)
        def _():
            for cp in copies(it + 1, 1 - slot):
                cp.start()

        for cp in copies(it, slot):
            cp.wait()
        consume(it // n_groups, it % n_groups, n_groups, slot)
        return carry

    lax.fori_loop(0, n_iter, body, 0)


def _pages(buf, slot, n_pages, dtype, axis):
    return jnp.concatenate([buf[slot, i].astype(dtype) for i in range(n_pages)], axis=axis)


def _softmax_update_t(s, vt, m_ref, l_ref, acc_ref):
    m_old = m_ref[...]
    m_new = jnp.maximum(m_old, jnp.max(s, axis=-1, keepdims=True))
    alpha = jnp.exp(m_old - m_new)
    p = jnp.exp(s - m_new)
    l_ref[...] = alpha * l_ref[...] + jnp.sum(p, axis=-1, keepdims=True)
    acc_ref[...] = alpha * acc_ref[...] + lax.dot_general(p.astype(BF16), vt, NT_DIMS,
                                                          preferred_element_type=F32)
    m_ref[...] = m_new


def _softmax_update(s, v, m_ref, l_ref, acc_ref):
    m_old = m_ref[...]
    m_new = jnp.maximum(m_old, jnp.max(s, axis=-1, keepdims=True))
    alpha = jnp.exp(m_old - m_new)
    p = jnp.exp(s - m_new)
    l_ref[...] = alpha * l_ref[...] + jnp.sum(p, axis=-1, keepdims=True)
    acc_ref[...] = alpha * acc_ref[...] + jnp.dot(p.astype(BF16), v, preferred_element_type=F32)
    m_ref[...] = m_new


def _new_row_update(s_new, m_ref, l_ref, acc_ref, v_new):
    m_old = m_ref[...]
    m_new = jnp.maximum(m_old, s_new)
    alpha = jnp.exp(m_old - m_new)
    p_new = jnp.exp(s_new - m_new)
    return (alpha * acc_ref[...] + p_new * v_new) / (alpha * l_ref[...] + p_new)


def _mla_decode_kernel(pt_ref, q_ref, ql_ref, qr_ref, knew_ref, wv_ref, ckv_hbm, kr_hbm, o_ref,
                       ckv_buf, kr_buf, sems, m_ref, l_ref, acc_ref, *, layer, n_pages):
    def consume(b, c, n_groups, slot):
        @pl.when(c == 0)
        def _():
            _init_stats(m_ref, l_ref, acc_ref)

        ckv_t = _pages(ckv_buf, slot, n_pages, BF16, 1)
        kr_t = _pages(kr_buf, slot, n_pages, BF16, 1)
        s = (jnp.dot(ql_ref[b], ckv_t, preferred_element_type=F32)
             + jnp.dot(qr_ref[b], kr_t, preferred_element_type=F32))
        _softmax_update_t(s, ckv_t, m_ref, l_ref, acc_ref)

        @pl.when(c == n_groups - 1)
        def _():
            knew = knew_ref[b]
            s_new = jnp.sum(q_ref[b].astype(F32) * knew, axis=-1, keepdims=True)
            o_lat = _new_row_update(s_new, m_ref, l_ref, acc_ref, knew[:, 0:MLA_LORA]).astype(BF16)
            for h in range(MLA_HEADS):
                o = jnp.dot(o_lat, wv_ref[h, 0:MLA_LORA, :], preferred_element_type=F32)
                o_ref[b, :, 128 * h:128 * (h + 1)] = o[h:h + 1, :]

    _paged_loop(pt_ref, layer, (ckv_hbm, kr_hbm), (ckv_buf, kr_buf), sems, n_pages, consume)


def _whole(a):
    return pl.BlockSpec(a.shape, lambda i, pt: (0,) * a.ndim)


def _decode_call(kernel_fn, name, page_table, vmem_inputs, caches, out_shape, scratch):
    n = PAGES_PER_STEP
    bufs = [pltpu.VMEM((2, n) + c.shape[2:], c.dtype) for c in caches]
    grid_spec = pltpu.PrefetchScalarGridSpec(
        num_scalar_prefetch=1,
        grid=(1,),
        in_specs=[_whole(a) for a in vmem_inputs] + [pl.BlockSpec(memory_space=pl.ANY)] * len(caches),
        out_specs=pl.BlockSpec(out_shape.shape, lambda i, pt: (0,) * len(out_shape.shape)),
        scratch_shapes=bufs + [pltpu.SemaphoreType.DMA((2, len(caches)))] + scratch,
    )
    return pl.pallas_call(kernel_fn, grid_spec=grid_spec, out_shape=out_shape,
                          compiler_params=_cparams(("arbitrary",)), name=name,
                          )(page_table, *vmem_inputs, *caches)


def mla_decode(page_table, q, knew, wv, cache_ckv_t, cache_kr_t, layer):
    nb, hp = q.shape[:2]
    ql, qr = q[:, :, :MLA_LORA], q[:, :, MLA_LORA:]
    return _decode_call(
        functools.partial(_mla_decode_kernel, layer=layer, n_pages=PAGES_PER_STEP), "mla_decode",
        page_table, [q, ql, qr, knew, wv], [cache_ckv_t, cache_kr_t],
        jax.ShapeDtypeStruct((nb, 1, MLA_HEADS * 128), F32),
        [pltpu.VMEM((hp, 1), F32), pltpu.VMEM((hp, 1), F32), pltpu.VMEM((hp, MLA_LORA), F32)])


def _diff_decode_kernel(pt_ref, q_ref, knew_ref, vnew_ref, lam_ref, g_ref, k_hbm, v_hbm, o_ref,
                        k_buf, v_buf, sems, m_ref, l_ref, acc_ref, *, layer, n_pages, lam_init):
    hp = q_ref.shape[1]
    low = lax.broadcasted_iota(jnp.int32, (1, LANES), 1) < DIFF_D
    zero = jnp.zeros((), BF16)

    def consume(b, c, n_groups, slot):
        qb = q_ref[b]
        q = jnp.concatenate([jnp.where(low, qb, zero), jnp.where(low, zero, qb)], axis=0)

        @pl.when(c == 0)
        def _():
            _init_stats(m_ref, l_ref, acc_ref)

        k_t = _pages(k_buf, slot, n_pages, BF16, 1)
        v = _pages(v_buf, slot, n_pages, BF16, 0)
        _softmax_update(jnp.dot(q, k_t, preferred_element_type=F32), v, m_ref, l_ref, acc_ref)

        @pl.when(c == n_groups - 1)
        def _():
            s_new = jnp.sum(q.astype(F32) * knew_ref[b], axis=-1, keepdims=True)
            o = _new_row_update(s_new, m_ref, l_ref, acc_ref, vnew_ref[b])
            o = o[:hp] - _diff_lambda(lam_ref, lam_init) * o[hp:]
            y = o * lax.rsqrt(jnp.mean(o * o, axis=-1, keepdims=True) + EPS)
            o_ref[b] = y * g_ref[...] * (1.0 - lam_init)

    _paged_loop(pt_ref, layer, (k_hbm, v_hbm), (k_buf, v_buf), sems, n_pages, consume)


def diff_decode(page_table, q, knew, vnew, lam_vecs, g, lam_init, cache_k_t, cache_v, layer):
    nb, hp = q.shape[:2]
    return _decode_call(
        functools.partial(_diff_decode_kernel, layer=layer, n_pages=PAGES_PER_STEP, lam_init=lam_init),
        "diff_decode", page_table, [q, knew, vnew, lam_vecs, g], [cache_k_t, cache_v],
        jax.ShapeDtypeStruct((nb, hp, 128), F32),
        [pltpu.VMEM((2 * hp, 1), F32), pltpu.VMEM((2 * hp, 1), F32), pltpu.VMEM((2 * hp, 128), F32)])


def _moba_decode_kernel(pt_ref, q_ref, knew_ref, vnew_ref, k_hbm, v_hbm, o_ref,
                        k_buf, v_buf, sems, km_ref, m_ref, l_ref, oblk_ref, *, layer, n_pages):
    bps = n_pages * PAGE // MOBA_BLOCK
    n_past_blocks = oblk_ref.shape[0]
    lane = lax.broadcasted_iota(jnp.int32, (1, LANES), 1)

    def consume(b, c, n_groups, slot):
        qf = q_ref[b]
        qb = (qf * QK_SCALE_64).astype(BF16)

        @pl.when(c == 0)
        def _():
            km_ref[...] = jnp.zeros(km_ref.shape, F32)
            m_ref[...] = jnp.zeros(m_ref.shape, F32)
            l_ref[...] = jnp.zeros(l_ref.shape, F32)

        kf_t =_pages(k_buf, slot, n_pages, F32, 1)
        v_t = _pages(v_buf, slot, n_pages, BF16, 1)
        s = jnp.dot(qb, kf_t.astype(BF16), preferred_element_type=F32)
        for j in range(bps):
            cols = slice(j * MOBA_BLOCK, (j + 1) * MOBA_BLOCK)
            here = lane == c * bps + j
            kmean = jnp.sum(kf_t[:, cols], axis=-1, keepdims=True) * (1.0 / MOBA_BLOCK)
            km_ref[...] = jnp.where(here, kmean, km_ref[...])
            sj = s[:, cols]
            mj = jnp.max(sj, axis=-1, keepdims=True)
            pj = jnp.exp(sj - mj)
            lj = jnp.sum(pj, axis=-1, keepdims=True)
            m_ref[...] = jnp.where(here, mj, m_ref[...])
            l_ref[...] = jnp.where(here, lj, l_ref[...])
            oblk_ref[c * bps + j] = lax.dot_general(pj.astype(BF16), v_t[:, cols], NT_DIMS,
                                                    preferred_element_type=F32)

        @pl.when(c == n_groups - 1)
        def _():
            gs = jnp.dot(qf, km_ref[...], precision=lax.Precision.HIGHEST,
                         preferred_element_type=F32)
            sel = _top3_mask(gs, lane, n_past_blocks, 1)
            s_new = jnp.sum(qf * QK_SCALE_64 * knew_ref[b], axis=-1, keepdims=True)
            mb = m_ref[...]
            m_all = jnp.maximum(jnp.max(jnp.where(sel > 0.0, mb, -jnp.inf), axis=-1, keepdims=True),
                                s_new)
            coef = jnp.where(sel > 0.0, jnp.exp(mb - m_all), 0.0)
            e_new = jnp.exp(s_new - m_all)
            denom = jnp.sum(coef * l_ref[...], axis=-1, keepdims=True) + e_new
            o = e_new * vnew_ref[b]
            for n in range(n_past_blocks):
                o = o + coef[:, n:n + 1] * oblk_ref[n]
            o_ref[b] = o / denom

    _paged_loop(pt_ref, layer, (k_hbm, v_hbm), (k_buf, v_buf), sems, n_pages, consume)


def moba_decode(page_table, q, knew, vnew, cache_k_t, cache_v_t, layer):
    nb, hp = q.shape[:2]
    nblk = page_table.shape[1] * PAGE // MOBA_BLOCK
    assert nblk <= LANES
    return _decode_call(
        functools.partial(_moba_decode_kernel, layer=layer, n_pages=PAGES_PER_STEP), "moba_decode",
        page_table, [q, knew, vnew], [cache_k_t, cache_v_t],
        jax.ShapeDtypeStruct((nb, hp, MOBA_HD), F32),
        [pltpu.VMEM((MOBA_HD, LANES), F32), pltpu.VMEM((hp, LANES), F32),
         pltpu.VMEM((hp, LANES), F32), pltpu.VMEM((nblk, hp, MOBA_HD), F32)])


def _rope_tables(pos):
    def blocks(rot_dim):
        half = rot_dim // 2
        inv = ROPE_THETA ** (-jnp.arange(half, dtype=F32) / half)
        ang = pos.astype(F32)[:, None] * inv[None, :]
        cos, sin = jnp.cos(ang), jnp.sin(ang)
        n = pos.shape[0]
        rest = 64 - rot_dim
        one, zero = jnp.ones((n, rest), F32), jnp.zeros((n, rest + half), F32)
        c = jnp.concatenate([cos, cos, one], axis=1)
        sa = jnp.concatenate([-sin, zero], axis=1)
        sb = jnp.concatenate([zero[:, :half], sin, zero[:, :rest]], axis=1)
        return [jnp.tile(t, (1, 2)) for t in (c, sa, sb)]
    return jnp.concatenate(blocks(16) + blocks(64), axis=1)


def _row_tile(n, target):
    return max(t for t in range(16, min(n, target) + 1, 16) if n % t == 0)


def _pad_axis(a, axis, size):
    pad = [(0, 0)] * a.ndim
    pad[axis] = (0, size - a.shape[axis])
    return jnp.pad(a, pad)


def _query_tiles(x, batch, nq, tq, heads, feat):
    x = x.reshape(batch, nq, tq, heads, feat)
    return jnp.transpose(x, (0, 1, 4, 3, 2)).reshape(batch, nq, feat, heads * tq)


def _untile_queries(x, batch, nq, tq, heads, feat):
    x = x.reshape(batch, nq, feat, heads, tq)
    return jnp.transpose(x, (0, 1, 4, 3, 2)).reshape(batch * nq * tq, heads * feat)


def _key_tiles(x, batch, nk, tk):
    x = x.reshape(batch, nk, tk, x.shape[-1])
    return x, jnp.swapaxes(x, 2, 3)


def kernel(x_prompt, x_sample, cache_mla_ckv, cache_mla_krope, cache_moba_k, cache_moba_v,
           cache_diff_k, cache_diff_v, page_table, ffn1_norm_g, ffn1_w_gate, ffn1_w_up, ffn1_w_down,
           mix_norm_g, w_in, mla_q_norm_g, mla_w_uq, mla_kv_norm_g, mla_w_ukv,
           diff_lambda_q1, diff_lambda_k1, diff_lambda_q2, diff_lambda_k2, diff_subln_g, w_out,
           ffn2_norm_g, ffn2_w_gate, ffn2_w_up, ffn2_w_down, final_norm_g):
    batch, seq, d_model = x_prompt.shape
    dec_b = x_sample.shape[0]
    depth = w_in.shape[0]
    n_prompt = batch * seq
    n_rows = n_prompt + dec_b
    past_len = page_table.shape[1] * PAGE
    n_pool = cache_mla_ckv.shape[1]
    tq, tk = Q_TILE, min(K_TILE, seq)
    nq, nk, nblk = seq // tq, seq // tk, seq // MOBA_BLOCK

    tm_big, tm_mid, tm_small = (_row_tile(n_rows, t) for t in (1040, 520, 320))

    x = jnp.concatenate([x_prompt.reshape(n_prompt, d_model), x_sample.reshape(dec_b, d_model)], axis=0)
    pos = jnp.concatenate([jnp.tile(jnp.arange(seq), batch), jnp.full((dec_b,), past_len)])
    tab = _rope_tables(pos)

    ck_mla = jnp.transpose(cache_mla_ckv, (0, 1, 3, 2))
    ck_kr = jnp.transpose(cache_mla_krope, (0, 1, 3, 2))
    ck_mk = jnp.transpose(cache_moba_k, (0, 1, 3, 4, 2)).reshape(depth, n_pool, MOBA_HD, PAGE)
    ck_mv = jnp.transpose(cache_moba_v, (0, 1, 3, 4, 2)).reshape(depth, n_pool, MOBA_HD, PAGE)
    ck_dk = jnp.transpose(cache_diff_k, (0, 1, 3, 4, 5, 2)).reshape(depth, n_pool, 2 * DIFF_D, PAGE)
    ck_dv = cache_diff_v.reshape(depth, n_pool, PAGE, 2 * DIFF_D)

    def ffn(x, g, wg, wu, wd):
        xn = rmsnorm(x, g, BF16, tm_small)
        h = ffn_gateup(xn, wg.astype(BF16), wu.astype(BF16), tm_big, 512)
        return ffn_down(h, wd.astype(BF16), x, tm_mid, 512)

    rows_out = [[] for _ in range(6)]
    for l in range(depth):
        x = ffn(x, ffn1_norm_g[l], ffn1_w_gate[l], ffn1_w_up[l], ffn1_w_down[l])

        hn = rmsnorm(x, mix_norm_g[l], BF16, tm_small)
        w_in_p = _pad_axis(w_in[l], 1, 4096).astype(BF16)
        z = matmul_f32out(hn, w_in_p, tm_big, 512)

        wuq = mla_w_uq[l].reshape(-1, MLA_HEADS, MLA_NOPE + MLA_ROPE)
        wuq_rope = jnp.pad(wuq[:, :, MLA_NOPE:], ((0, 0), (0, 0), (64, 0)))
        wuq_p = jnp.concatenate([wuq[:, :, :MLA_NOPE].reshape(-1, MLA_HEADS * 128),
                                 wuq_rope.reshape(-1, MLA_HEADS * 128)], axis=1).astype(BF16)
        wukv = mla_w_ukv[l].reshape(MLA_LORA, MLA_HEADS, MLA_NOPE + 128)
        wk = _pad_axis(jnp.transpose(wukv[:, :, :MLA_NOPE], (1, 2, 0)), 2, 256).astype(BF16)
        wv = _pad_axis(jnp.transpose(wukv[:, :, MLA_NOPE:], (1, 0, 2)), 1, 256).astype(BF16)
        wv_t = jnp.swapaxes(wv, 1, 2)
        gq = mla_q_norm_g[l].reshape(1, -1)
        gkv = _pad_axis(mla_kv_norm_g[l].reshape(1, -1), 1, 256)

        qm, kv, mq, mkv, dq, dkv = in_proj_post(z, tab, gq, gkv, wuq_p, wk, tm_small)
        r_mk, r_mv = mkv[:, :MOBA_HD], mkv[:, MOBA_HD:]
        lam_vecs = jnp.stack([diff_lambda_q1[l], diff_lambda_k1[l], diff_lambda_q2[l], diff_lambda_k2[l]])
        lam_init = 0.8 - 0.6 * math.exp(-0.3 * l)
        g_sub = diff_subln_g[l].reshape(1, -1)

        mla_k, mla_kt = _key_tiles(kv[:n_prompt].astype(BF16), batch, nk, tk)
        oa_p = mla_prompt(_query_tiles(qm[:n_prompt], batch, nq, tq, MLA_HEADS, 256),
                          mla_k, mla_kt, wv_t, tq, tk)
        oa_p = jnp.transpose(oa_p.reshape(batch, nq, MLA_HEADS * 128, tq), (0, 1, 3, 2))
        oa_p = oa_p.reshape(n_prompt, MLA_HEADS * 128)

        diff_k, _ = _key_tiles(dkv[:n_prompt, :128].astype(BF16), batch, nk, tk)
        _, diff_vt = _key_tiles(dkv[:n_prompt, 128:].astype(BF16), batch, nk, tk)
        oc_p = diff_prompt(_query_tiles(dq[:n_prompt], batch, nq, tq, DIFF_HEADS, 128),
                           diff_k, diff_vt, lam_vecs, g_sub.reshape(-1, 1), lam_init, tq, tk)
        oc_p = _untile_queries(oc_p, batch, nq, tq, DIFF_HEADS, 128)

        moba_k, _ = _key_tiles(r_mk[:n_prompt], batch, nblk, MOBA_BLOCK)
        _, moba_vt = _key_tiles(r_mv[:n_prompt].astype(BF16), batch, nblk, MOBA_BLOCK)
        ob_p = moba_prompt(_query_tiles(mq[:n_prompt], batch, nq, tq, MOBA_HEADS, MOBA_HD),
                           moba_k, moba_vt, tq)
        ob_p = _untile_queries(ob_p, batch, nq, tq, MOBA_HEADS, MOBA_HD)

        qm_d = _pad_axis(qm[n_prompt:].reshape(dec_b, MLA_HEADS, 256), 1, 16)
        oa_d = mla_decode(page_table, qm_d, kv[n_prompt:].reshape(dec_b, 1, 256), wv,
                          ck_mla, ck_kr, l).reshape(dec_b, MLA_HEADS * 128)
        dq_d = _pad_axis(dq[n_prompt:].reshape(dec_b, DIFF_HEADS, 128), 1, 16)
        oc_d = diff_decode(page_table, dq_d, dkv[n_prompt:, :128].reshape(dec_b, 1, 128),
                           dkv[n_prompt:, 128:].reshape(dec_b, 1, 128), lam_vecs, g_sub, lam_init,
                           ck_dk, ck_dv, l)
        oc_d = oc_d[:, :DIFF_HEADS].reshape(dec_b, DIFF_HEADS * 128)
        mq_d = _pad_axis(mq[n_prompt:].reshape(dec_b, MOBA_HEADS, MOBA_HD), 1, 32)
        ob_d = moba_decode(page_table, mq_d, r_mk[n_prompt:].reshape(dec_b, 1, MOBA_HD),
                           r_mv[n_prompt:].reshape(dec_b, 1, MOBA_HD), ck_mk, ck_mv, l)
        ob_d = ob_d[:, :MOBA_HEADS].reshape(dec_b, MOBA_HEADS * MOBA_HD)

        oa = jnp.concatenate([oa_p, oa_d.astype(BF16)], axis=0)
        ob = jnp.concatenate([ob_p, ob_d.astype(BF16)], axis=0)
        oc = jnp.concatenate([oc_p, oc_d.astype(BF16)], axis=0)
        wo = w_out[l].astype(BF16)
        n_a, n_b = MLA_HEADS * 128, MOBA_HEADS * MOBA_HD
        x = out_proj(oa, ob, oc, wo[:n_a], wo[n_a:n_a + n_b], wo[n_a + n_b:], x, tm_big, 512)

        x = ffn(x, ffn2_norm_g[l], ffn2_w_gate[l], ffn2_w_up[l], ffn2_w_down[l])

        for acc, r in zip(rows_out, (kv[:, :MLA_LORA], kv[:, MLA_LORA:], r_mk, r_mv,
                                     dkv[:, :128], dkv[:, 128:])):
            acc.append(r)

    y = rmsnorm(x, final_norm_g, F32, tm_small)
    y_prompt = y[:n_prompt].reshape(batch, seq, d_model)
    y_sample = y[n_prompt:].reshape(dec_b, 1, d_model)

    tails = ((MLA_LORA,), (MLA_ROPE,), (1, MOBA_HD), (1, MOBA_HD), (1, 2, DIFF_D), (1, 2 * DIFF_D))
    stacked = [jnp.stack(a) for a in rows_out]
    prompt_rows = tuple(s[:, :n_prompt].reshape((depth, batch, seq) + t) for s, t in zip(stacked, tails))
    sample_rows = tuple(s[:, n_prompt:].reshape((depth, dec_b, 1) + t) for s, t in zip(stacked, tails))
    return (y_prompt, y_sample) + prompt_rows + sample_rows
```

```python
import functools
import math

import jax
import jax.numpy as jnp
from jax import lax
from jax.experimental import pallas as pl
from jax.experimental.pallas import tpu as pltpu

F32 = jnp.float32
BF16 = jnp.bfloat16

EPS = 1e-6
ROPE_THETA = 500000.0
FFN_RES_WEIGHT = 0.5
MLA_HEADS = 12
MLA_NOPE = 128
MLA_ROPE = 64
MLA_LORA = 192
MLA_SCALE = (MLA_NOPE + MLA_ROPE) ** -0.5
MOBA_HEADS = 20
MOBA_HD = 64
MOBA_BLOCK = 256
MOBA_TOPK = 3
DIFF_HEADS = 10
DIFF_D = 64
QK_SCALE_64 = 0.125
PAGE = 128
PAGES_PER_STEP = 32
PAGE_SLOTS = 3
Q_TILE = 128
K_TILE = 512
LANES = 128
VMEM_LIMIT = 56 * 1024 * 1024

NT_DIMS = (((1,), (1,)), ((), ()))


def _cparams(sem):
    return pltpu.CompilerParams(dimension_semantics=sem, vmem_limit_bytes=VMEM_LIMIT)


def _rmsnorm_kernel(x_ref, g_ref, o_ref):
    x = x_ref[...]
    y = x * lax.rsqrt(jnp.mean(x * x, axis=-1, keepdims=True) + EPS)
    o_ref[...] = (y * g_ref[...]).astype(o_ref.dtype)


def rmsnorm(x, g, out_dtype, tm):
    n, d = x.shape
    return pl.pallas_call(
        _rmsnorm_kernel,
        grid=(n // tm,),
        in_specs=[pl.BlockSpec((tm, d), lambda i: (i, 0)),
                  pl.BlockSpec((1, d), lambda i: (0, 0))],
        out_specs=pl.BlockSpec((tm, d), lambda i: (i, 0)),
        out_shape=jax.ShapeDtypeStruct((n, d), out_dtype),
        compiler_params=_cparams(("parallel",)),
        name="rmsnorm",
    )(x, g.reshape(1, d))


def _gateup_kernel(x_ref, wg_ref, wu_ref, o_ref):
    x = x_ref[...]
    g = jnp.dot(x, wg_ref[...], preferred_element_type=F32)
    u = jnp.dot(x, wu_ref[...], preferred_element_type=F32)
    o_ref[...] = (g * jax.nn.sigmoid(g) * u).astype(o_ref.dtype)


def _layer_cols(layer, k, tn):
    return pl.BlockSpec((None, k, tn), lambda i, j: (layer, 0, j))


def ffn_gateup(xn, wg, wu, layer, tm, tn):
    n, d = xn.shape
    f = wg.shape[2]
    return pl.pallas_call(
        _gateup_kernel,
        grid=(n // tm, f // tn),
        in_specs=[pl.BlockSpec((tm, d), lambda i, j: (i, 0)),
                  _layer_cols(layer, d, tn), _layer_cols(layer, d, tn)],
        out_specs=pl.BlockSpec((tm, tn), lambda i, j: (i, j)),
        out_shape=jax.ShapeDtypeStruct((n, f), BF16),
        compiler_params=_cparams(("parallel", "arbitrary")),
        name="ffn_gateup",
    )(xn, wg, wu)


def _down_kernel(h_ref, w_ref, x_ref, o_ref):
    y = jnp.dot(h_ref[...], w_ref[...], preferred_element_type=F32)
    o_ref[...] = x_ref[...] + FFN_RES_WEIGHT * y


def ffn_down(h, wd, x, layer, tm, tn):
    n, f = h.shape
    d = wd.shape[2]
    return pl.pallas_call(
        _down_kernel,
        grid=(n // tm, d // tn),
        in_specs=[pl.BlockSpec((tm, f), lambda i, j: (i, 0)),
                  _layer_cols(layer, f, tn),
                  pl.BlockSpec((tm, tn), lambda i, j: (i, j))],
        out_specs=pl.BlockSpec((tm, tn), lambda i, j: (i, j)),
        out_shape=jax.ShapeDtypeStruct((n, d), F32),
        compiler_params=_cparams(("parallel", "arbitrary")),
        name="ffn_down",
    )(h, wd, x)


def _matmul_kernel(x_ref, w_ref, o_ref):
    o_ref[...] = jnp.dot(x_ref[...], w_ref[...], preferred_element_type=F32)


def matmul_f32out(x, w, layer, tm, tn):
    n, d = x.shape
    f = w.shape[2]
    return pl.pallas_call(
        _matmul_kernel,
        grid=(n // tm, f // tn),
        in_specs=[pl.BlockSpec((tm, d), lambda i, j: (i, 0)),
                  _layer_cols(layer, d, tn)],
        out_specs=pl.BlockSpec((tm, tn), lambda i, j: (i, j)),
        out_shape=jax.ShapeDtypeStruct((n, f), F32),
        compiler_params=_cparams(("parallel", "arbitrary")),
        name="in_proj",
    )(x, w)


def _outproj_kernel(a_ref, w_ref, x_ref, o_ref):
    o_ref[...] = x_ref[...] + jnp.dot(a_ref[...], w_ref[...], preferred_element_type=F32)


def out_proj(a, w, x, layer, tm, tn):
    n, k = a.shape
    d = w.shape[2]
    return pl.pallas_call(
        _outproj_kernel,
        grid=(n // tm, d // tn),
        in_specs=[pl.BlockSpec((tm, k), lambda i, j: (i, 0)),
                  _layer_cols(layer, k, tn),
                  pl.BlockSpec((tm, tn), lambda i, j: (i, j))],
        out_specs=pl.BlockSpec((tm, tn), lambda i, j: (i, j)),
        out_shape=jax.ShapeDtypeStruct((n, d), F32),
        compiler_params=_cparams(("parallel", "arbitrary")),
        name="out_proj",
    )(a, w, x)


_C_CQ, _C_KV, _C_MQ, _C_MKV, _C_DQ, _C_DK, _C_DV = 0, 768, 1024, 2304, 2432, 3712, 3840
_Q_ROPE0 = MLA_HEADS * MLA_NOPE


def _rot(blk, c, sa, sb, half):
    return (blk * c + pltpu.roll(blk, LANES - half, 1) * sa + pltpu.roll(blk, half, 1) * sb)


def _post_kernel(z_ref, tab_ref, gq_ref, gkv_ref, wuq_ref, wk_ref,
                 qm_ref, kv_ref, mq_ref, mkv_ref, dq_ref, dkv_ref):
    c16, a16, b16 = tab_ref[:, 0:128], tab_ref[:, 128:256], tab_ref[:, 256:384]
    c64, a64, b64 = tab_ref[:, 384:512], tab_ref[:, 512:640], tab_ref[:, 640:768]
    rot16 = lambda blk: _rot(blk, c16, a16, b16, 8)
    rot64 = lambda blk: _rot(blk, c64, a64, b64, 32)
    lane = lax.broadcasted_iota(jnp.int32, (1, LANES), 1)
    low = lane < 64

    cq = z_ref[:, _C_CQ:_C_CQ + 768]
    cqn = cq * lax.rsqrt(jnp.mean(cq * cq, axis=-1, keepdims=True) + EPS) * gq_ref[...]
    q = jnp.dot(cqn.astype(BF16), wuq_ref[...], preferred_element_type=F32)
    for h in range(MLA_HEADS):
        ql = jnp.dot(q[:, 128 * h:128 * (h + 1)].astype(BF16), wk_ref[h],
                     preferred_element_type=F32)
        rb = rot64(q[:, _Q_ROPE0 + 128 * h:_Q_ROPE0 + 128 * (h + 1)])
        qm_ref[:, 256 * h:256 * h + 128] = (ql[:, :128] * MLA_SCALE).astype(BF16)
        qm_ref[:, 256 * h + 128:256 * h + 256] = ((ql[:, 128:] + rb) * MLA_SCALE).astype(BF16)

    k0 = z_ref[:, _C_KV:_C_KV + 128]
    k1 = z_ref[:, _C_KV + 128:_C_KV + 256]
    ss = (jnp.sum(k0 * k0, axis=-1, keepdims=True)
          + jnp.sum(jnp.where(low, k1 * k1, 0.0), axis=-1, keepdims=True))
    r = lax.rsqrt(ss * (1.0 / MLA_LORA) + EPS)
    kv_ref[:, 0:128] = k0 * r * gkv_ref[:, 0:128]
    kv_ref[:, 128:256] = jnp.where(low, k1 * r * gkv_ref[:, 128:256], rot64(k1))

    for j in range(MOBA_HEADS // 2):
        mq_ref[:, 128 * j:128 * (j + 1)] = rot16(z_ref[:, _C_MQ + 128 * j:_C_MQ + 128 * (j + 1)])
    mkv = z_ref[:, _C_MKV:_C_MKV + 128]
    mkv_ref[...] = jnp.where(low, rot16(mkv), mkv)

    for j in range(DIFF_HEADS):
        blk = rot16(z_ref[:, _C_DQ + 128 * j:_C_DQ + 128 * (j + 1)])
        dq_ref[:, 128 * j:128 * (j + 1)] = (blk * QK_SCALE_64).astype(BF16)
    dkv_ref[:, 0:128] = rot16(z_ref[:, _C_DK:_C_DK + 128])
    dkv_ref[:, 128:256] = z_ref[:, _C_DV:_C_DV + 128]


def in_proj_post(z, tab, gq, gkv, wuq, wk, tm):
    n, zc = z.shape
    row = lambda k: pl.BlockSpec((tm, k), lambda i: (i, 0))
    full = lambda a: pl.BlockSpec(a.shape, lambda i: (0,) * a.ndim)
    out_shapes = (
        jax.ShapeDtypeStruct((n, MLA_HEADS * 256), BF16),
        jax.ShapeDtypeStruct((n, 256), F32),
        jax.ShapeDtypeStruct((n, 1280), F32),
        jax.ShapeDtypeStruct((n, 128), F32),
        jax.ShapeDtypeStruct((n, 1280), BF16),
        jax.ShapeDtypeStruct((n, 256), F32),
    )
    return pl.pallas_call(
        _post_kernel,
        grid=(n // tm,),
        in_specs=[row(zc), row(tab.shape[1]), full(gq), full(gkv), full(wuq), full(wk)],
        out_specs=tuple(row(s.shape[1]) for s in out_shapes),
        out_shape=out_shapes,
        compiler_params=_cparams(("parallel",)),
        name="in_proj_post",
    )(z, tab, gq, gkv, wuq, wk)


def _softmax_update_cols(s_t, v_t, m_ref, l_ref, acc_ref):
    m_old = m_ref[...]
    m_new = jnp.maximum(m_old, jnp.max(s_t, axis=0, keepdims=True))
    alpha = jnp.exp(m_old - m_new)
    p_t = jnp.exp(s_t - m_new)
    l_ref[...] = alpha * l_ref[...] + jnp.sum(p_t, axis=0, keepdims=True)
    acc_ref[...] = alpha * acc_ref[...] + jnp.dot(v_t, p_t.astype(BF16), preferred_element_type=F32)
    m_ref[...] = m_new


def _causal_cols(s_t, key0, q0, tq):
    kpos = key0 + lax.broadcasted_iota(jnp.int32, (s_t.shape[0], 1), 0)
    qpos = q0 + lax.broadcasted_iota(jnp.int32, (1, s_t.shape[1]), 1) % tq
    return jnp.where(kpos <= qpos, s_t, -jnp.inf)


def _init_stats(m_ref, l_ref, acc_ref):
    m_ref[...] = jnp.full(m_ref.shape, -jnp.inf, F32)
    l_ref[...] = jnp.zeros(l_ref.shape, F32)
    acc_ref[...] = jnp.zeros(acc_ref.shape, F32)


def _mla_prompt_kernel(q_ref, k_ref, kt_ref, wvt_ref, o_ref, m_ref, l_ref, acc_ref, *, tq, tk):
    qt = pl.program_id(1)
    q_t = q_ref[...]
    _init_stats(m_ref, l_ref, acc_ref)

    def body(c, carry):
        s_t = jnp.dot(k_ref[c], q_t, preferred_element_type=F32)
        s_t = _causal_cols(s_t, c * tk, qt * tq, tq)
        _softmax_update_cols(s_t, kt_ref[c], m_ref, l_ref, acc_ref)
        return carry

    lax.fori_loop(0, (qt * tq + tq + tk - 1) // tk, body, 0)
    o_lat = (acc_ref[...] / l_ref[...]).astype(BF16)
    for h in range(MLA_HEADS):
        o_ref[128 * h:128 * (h + 1), :] = jnp.dot(
            wvt_ref[h], o_lat[:, h * tq:(h + 1) * tq], preferred_element_type=F32).astype(o_ref.dtype)


def mla_prompt(q_t, k, k_t, wv_t, tq, tk):
    batch, nq = q_t.shape[:2]
    cols = MLA_HEADS * tq
    return pl.pallas_call(
        functools.partial(_mla_prompt_kernel, tq=tq, tk=tk),
        grid=(batch, nq),
        in_specs=[pl.BlockSpec((None, None, 256, cols), lambda b, i: (b, i, 0, 0)),
                  pl.BlockSpec((None,) + k.shape[1:], lambda b, i: (b, 0, 0, 0)),
                  pl.BlockSpec((None,) + k_t.shape[1:], lambda b, i: (b, 0, 0, 0)),
                  pl.BlockSpec(wv_t.shape, lambda b, i: (0, 0, 0))],
        out_specs=pl.BlockSpec((None, None, MLA_HEADS * 128, tq), lambda b, i: (b, i, 0, 0)),
        out_shape=jax.ShapeDtypeStruct((batch, nq, MLA_HEADS * 128, tq), BF16),
        scratch_shapes=[pltpu.VMEM((1, cols), F32), pltpu.VMEM((1, cols), F32),
                        pltpu.VMEM((256, cols), F32)],
        compiler_params=_cparams(("parallel", "arbitrary")),
        name="mla_prompt",
    )(q_t, k, k_t, wv_t)


def _diff_lambda(lam_ref, lam_init):
    e1 = jnp.exp(jnp.sum(lam_ref[0:1, :] * lam_ref[1:2, :], axis=-1, keepdims=True))
    e2 = jnp.exp(jnp.sum(lam_ref[2:3, :] * lam_ref[3:4, :], axis=-1, keepdims=True))
    return e1 - e2 + lam_init


def _diff_prompt_kernel(q_ref, k_ref, vt_ref, lam_ref, g_ref, o_ref, m_ref, l_ref, acc_ref,
                        *, tq, tk, lam_init):
    qt = pl.program_id(1)
    half = DIFF_HEADS * tq
    first = lax.broadcasted_iota(jnp.int32, (2 * DIFF_D, 1), 0) < DIFF_D
    zero = jnp.zeros((), BF16)
    q = q_ref[...]
    q_t = jnp.concatenate([jnp.where(first, q, zero), jnp.where(first, zero, q)], axis=1)
    _init_stats(m_ref, l_ref, acc_ref)

    def body(c, carry):
        s_t = jnp.dot(k_ref[c], q_t, preferred_element_type=F32)
        s_t = _causal_cols(s_t, c * tk, qt * tq, tq)
        _softmax_update_cols(s_t, vt_ref[c], m_ref, l_ref, acc_ref)
        return carry

    lax.fori_loop(0, (qt * tq + tq + tk - 1) // tk, body, 0)
    o = acc_ref[...] / l_ref[...]
    o = o[:, :half] - _diff_lambda(lam_ref, lam_init) * o[:, half:]
    y = o * lax.rsqrt(jnp.mean(o * o, axis=0, keepdims=True) + EPS)
    o_ref[...] = (y * g_ref[...] * (1.0 - lam_init)).astype(o_ref.dtype)


def diff_prompt(q_t, k, v_t, lam_vecs, g_col, lam_init, tq, tk):
    batch, nq = q_t.shape[:2]
    cols = DIFF_HEADS * tq
    return pl.pallas_call(
        functools.partial(_diff_prompt_kernel, tq=tq, tk=tk, lam_init=lam_init),
        grid=(batch, nq),
        in_specs=[pl.BlockSpec((None, None, 128, cols), lambda b, i: (b, i, 0, 0)),
                  pl.BlockSpec((None,) + k.shape[1:], lambda b, i: (b, 0, 0, 0)),
                  pl.BlockSpec((None,) + v_t.shape[1:], lambda b, i: (b, 0, 0, 0)),
                  pl.BlockSpec(lam_vecs.shape, lambda b, i: (0, 0)),
                  pl.BlockSpec(g_col.shape, lambda b, i: (0, 0))],
        out_specs=pl.BlockSpec((None, None, 128, cols), lambda b, i: (b, i, 0, 0)),
        out_shape=jax.ShapeDtypeStruct((batch, nq, 128, cols), BF16),
        scratch_shapes=[pltpu.VMEM((1, 2 * cols), F32), pltpu.VMEM((1, 2 * cols), F32),
                        pltpu.VMEM((128, 2 * cols), F32)],
        compiler_params=_cparams(("parallel", "arbitrary")),
        name="diff_prompt",
    )(q_t, k, v_t, lam_vecs, g_col)


def _top3_mask(gs, blk, n_valid, axis):
    blk = blk.astype(F32)
    valid = blk < jnp.asarray(n_valid, F32)
    gs = jnp.where(valid, gs, -jnp.inf)
    sentinel = float(gs.shape[axis])
    sel = jnp.zeros(gs.shape, F32)
    for _ in range(MOBA_TOPK):
        mx = jnp.max(gs, axis=axis, keepdims=True)
        idx = jnp.min(jnp.where(gs == mx, blk, sentinel), axis=axis, keepdims=True)
        pick = blk == idx
        sel = jnp.where(pick, 1.0, sel)
        gs = jnp.where(pick, -jnp.inf, gs)
    return jnp.where(valid, sel, 0.0)


def _moba_prompt_kernel(q_ref, k_ref, vt_ref, o_ref, m_ref, l_ref, acc_ref, *, tq):
    qt = pl.program_id(1)
    nblk = k_ref.shape[0]
    cur = (qt * tq) // MOBA_BLOCK
    qf = q_ref[...]
    qb = (qf * QK_SCALE_64).astype(BF16)

    kmean = jnp.sum(k_ref[...], axis=1) * (1.0 / MOBA_BLOCK)
    gs = jnp.dot(kmean, qf, precision=lax.Precision.HIGHEST, preferred_element_type=F32)
    blk = lax.broadcasted_iota(jnp.int32, (nblk, 1), 0)
    sel = _top3_mask(gs, blk, cur, 0)

    s_t = jnp.dot(k_ref[cur].astype(BF16), qb, preferred_element_type=F32)
    s_t = _causal_cols(s_t, cur * MOBA_BLOCK, qt * tq, tq)
    m0 = jnp.max(s_t, axis=0, keepdims=True)
    p_t = jnp.exp(s_t - m0)
    m_ref[...] = m0
    l_ref[...] = jnp.sum(p_t, axis=0, keepdims=True)
    acc_ref[...] = jnp.dot(vt_ref[cur], p_t.astype(BF16), preferred_element_type=F32)

    for n in range(nblk - 1):
        @pl.when(n < cur)
        def _():
            sn = jnp.dot(k_ref[n].astype(BF16), qb, preferred_element_type=F32)
            sn = jnp.where(sel[n:n + 1, :] > 0.0, sn, -jnp.inf)
            _softmax_update_cols(sn, vt_ref[n], m_ref, l_ref, acc_ref)

    o_ref[...] = (acc_ref[...] / l_ref[...]).astype(o_ref.dtype)


def moba_prompt(q_t, k, v_t, tq):
    batch, nq = q_t.shape[:2]
    cols = MOBA_HEADS * tq
    return pl.pallas_call(
        functools.partial(_moba_prompt_kernel, tq=tq),
        grid=(batch, nq),
        in_specs=[pl.BlockSpec((None, None, MOBA_HD, cols), lambda b, i: (b, i, 0, 0)),
                  pl.BlockSpec((None,) + k.shape[1:], lambda b, i: (b, 0, 0, 0)),
                  pl.BlockSpec((None,) + v_t.shape[1:], lambda b, i: (b, 0, 0, 0))],
        out_specs=pl.BlockSpec((None, None, MOBA_HD, cols), lambda b, i: (b, i, 0, 0)),
        out_shape=jax.ShapeDtypeStruct((batch, nq, MOBA_HD, cols), BF16),
        scratch_shapes=[pltpu.VMEM((1, cols), F32), pltpu.VMEM((1, cols), F32),
                        pltpu.VMEM((MOBA_HD, cols), F32)],
        compiler_params=_cparams(("parallel", "arbitrary")),
        name="moba_prompt",
    )(q_t, k, v_t)


def _paged_loop(pt_ref, layer, caches, bufs, sems, n_pages, consume):
    nb, n_tab = pt_ref.shape
    n_groups = n_tab // n_pages
    n_iter = nb * n_groups
    n_slots = bufs[0].shape[0]
    ahead = n_slots - 1
    assert n_iter >= ahead

    def copies(it):
        b, c, slot = it // n_groups, it % n_groups, it % n_slots
        out = []
        for i in range(n_pages):
            page = pt_ref[b, c * n_pages + i]
            for j, (cache, buf) in enumerate(zip(caches, bufs)):
                out.append(pltpu.make_async_copy(cache.at[layer, page], buf.at[slot, i], sems.at[slot, j]))
        return out

    for it in range(ahead):
        for cp in copies(it):
            cp.start()

    def body(it, carry):
        @pl.when(it + ahead < n_iter)
        def _():
            for cp in copies(it + ahead):
                cp.start()

        for cp in copies(it):
            cp.wait()
        consume(it // n_groups, it % n_groups, n_groups, it % n_slots)
        return carry

    lax.fori_loop(0, n_iter, body, 0)


def _pages(buf, slot, n_pages, dtype, axis):
    return jnp.concatenate([buf[slot, i].astype(dtype) for i in range(n_pages)], axis=axis)


def _softmax_update_t(s, vt, m_ref, l_ref, acc_ref):
    m_old = m_ref[...]
    m_new = jnp.maximum(m_old, jnp.max(s, axis=-1, keepdims=True))
    alpha = jnp.exp(m_old - m_new)
    p = jnp.exp(s - m_new)
    l_ref[...] = alpha * l_ref[...] + jnp.sum(p, axis=-1, keepdims=True)
    acc_ref[...] = alpha * acc_ref[...] + lax.dot_general(p.astype(BF16), vt, NT_DIMS,
                                                          preferred_element_type=F32)
    m_ref[...] = m_new


def _softmax_update(s, v, m_ref, l_ref, acc_ref):
    m_old = m_ref[...]
    m_new = jnp.maximum(m_old, jnp.max(s, axis=-1, keepdims=True))
    alpha = jnp.exp(m_old - m_new)
    p = jnp.exp(s - m_new)
    l_ref[...] = alpha * l_ref[...] + jnp.sum(p, axis=-1, keepdims=True)
    acc_ref[...] = alpha * acc_ref[...] + jnp.dot(p.astype(BF16), v, preferred_element_type=F32)
    m_ref[...] = m_new


def _new_row_update(s_new, m_ref, l_ref, acc_ref, v_new):
    m_old = m_ref[...]
    m_new = jnp.maximum(m_old, s_new)
    alpha = jnp.exp(m_old - m_new)
    p_new = jnp.exp(s_new - m_new)
    return (alpha * acc_ref[...] + p_new * v_new) / (alpha * l_ref[...] + p_new)


def _mla_decode_kernel(pt_ref, q_ref, ql_ref, qr_ref, knew_ref, wv_ref, ckv_hbm, kr_hbm, o_ref,
                       ckv_buf, kr_buf, sems, m_ref, l_ref, acc_ref, *, layer, n_pages):
    def consume(b, c, n_groups, slot):
        @pl.when(c == 0)
        def _():
            _init_stats(m_ref, l_ref, acc_ref)

        ckv_t = _pages(ckv_buf, slot, n_pages, BF16, 1)
        kr_t = _pages(kr_buf, slot, n_pages, BF16, 1)
        s = (jnp.dot(ql_ref[b], ckv_t, preferred_element_type=F32)
             + jnp.dot(qr_ref[b], kr_t, preferred_element_type=F32))
        _softmax_update_t(s, ckv_t, m_ref, l_ref, acc_ref)

        @pl.when(c == n_groups - 1)
        def _():
            knew = knew_ref[b]
            s_new = jnp.sum(q_ref[b].astype(F32) * knew, axis=-1, keepdims=True)
            o_lat = _new_row_update(s_new, m_ref, l_ref, acc_ref, knew[:, 0:MLA_LORA]).astype(BF16)
            for h in range(MLA_HEADS):
                o = jnp.dot(o_lat, wv_ref[h, 0:MLA_LORA, :], preferred_element_type=F32)
                o_ref[b, :, 128 * h:128 * (h + 1)] = o[h:h + 1, :]

    _paged_loop(pt_ref, layer, (ckv_hbm, kr_hbm), (ckv_buf, kr_buf), sems, n_pages, consume)


def _whole(a):
    return pl.BlockSpec(a.shape, lambda i, pt: (0,) * a.ndim)


def _decode_call(kernel_fn, name, page_table, vmem_inputs, caches, out_shape, scratch):
    bufs = [pltpu.VMEM((PAGE_SLOTS, PAGES_PER_STEP) + c.shape[2:], c.dtype) for c in caches]
    grid_spec = pltpu.PrefetchScalarGridSpec(
        num_scalar_prefetch=1,
        grid=(1,),
        in_specs=[_whole(a) for a in vmem_inputs] + [pl.BlockSpec(memory_space=pl.ANY)] * len(caches),
        out_specs=pl.BlockSpec(out_shape.shape, lambda i, pt: (0,) * len(out_shape.shape)),
        scratch_shapes=bufs + [pltpu.SemaphoreType.DMA((PAGE_SLOTS, len(caches)))] + scratch,
    )
    return pl.pallas_call(kernel_fn, grid_spec=grid_spec, out_shape=out_shape,
                          compiler_params=_cparams(("arbitrary",)), name=name,
                          )(page_table, *vmem_inputs, *caches)


def mla_decode(page_table, q, knew, wv, cache_ckv_t, cache_kr_t, layer):
    nb, hp = q.shape[:2]
    ql, qr = q[:, :, :MLA_LORA], q[:, :, MLA_LORA:]
    return _decode_call(
        functools.partial(_mla_decode_kernel, layer=layer, n_pages=PAGES_PER_STEP), "mla_decode",
        page_table, [q, ql, qr, knew, wv], [cache_ckv_t, cache_kr_t],
        jax.ShapeDtypeStruct((nb, 1, MLA_HEADS * 128), F32),
        [pltpu.VMEM((hp, 1), F32), pltpu.VMEM((hp, 1), F32), pltpu.VMEM((hp, MLA_LORA), F32)])


def _diff_decode_kernel(pt_ref, q_ref, knew_ref, vnew_ref, lam_ref, g_ref, k_hbm, v_hbm, o_ref,
                        k_buf, v_buf, sems, m_ref, l_ref, acc_ref, *, layer, n_pages, lam_init):
    hp = q_ref.shape[1]
    low = lax.broadcasted_iota(jnp.int32, (1, LANES), 1) < DIFF_D
    zero = jnp.zeros((), BF16)

    def consume(b, c, n_groups, slot):
        qb = q_ref[b]
        q = jnp.concatenate([jnp.where(low, qb, zero), jnp.where(low, zero, qb)], axis=0)

        @pl.when(c == 0)
        def _():
            _init_stats(m_ref, l_ref, acc_ref)

        k_t = _pages(k_buf, slot, n_pages, BF16, 1)
        v = _pages(v_buf, slot, n_pages, BF16, 0)
        _softmax_update(jnp.dot(q, k_t, preferred_element_type=F32), v, m_ref, l_ref, acc_ref)

        @pl.when(c == n_groups - 1)
        def _():
            s_new = jnp.sum(q.astype(F32) * knew_ref[b], axis=-1, keepdims=True)
            o = _new_row_update(s_new, m_ref, l_ref, acc_ref, vnew_ref[b])
            o = o[:hp] - _diff_lambda(lam_ref, lam_init) * o[hp:]
            y = o * lax.rsqrt(jnp.mean(o * o, axis=-1, keepdims=True) + EPS)
            o_ref[b] = y * g_ref[...] * (1.0 - lam_init)

    _paged_loop(pt_ref, layer, (k_hbm, v_hbm), (k_buf, v_buf), sems, n_pages, consume)


def diff_decode(page_table, q, knew, vnew, lam_vecs, g, lam_init, cache_k_t, cache_v, layer):
    nb, hp = q.shape[:2]
    return _decode_call(
        functools.partial(_diff_decode_kernel, layer=layer, n_pages=PAGES_PER_STEP, lam_init=lam_init),
        "diff_decode", page_table, [q, knew, vnew, lam_vecs, g], [cache_k_t, cache_v],
        jax.ShapeDtypeStruct((nb, hp, 128), F32),
        [pltpu.VMEM((2 * hp, 1), F32), pltpu.VMEM((2 * hp, 1), F32), pltpu.VMEM((2 * hp, 128), F32)])


def _moba_decode_kernel(pt_ref, q_ref, knew_ref, vnew_ref, k_hbm, v_hbm, o_ref,
                        k_buf, v_buf, sems, km_ref, m_ref, l_ref, oblk_ref, *, layer, n_pages):
    bps = n_pages * PAGE // MOBA_BLOCK
    n_past_blocks = oblk_ref.shape[0]
    lane = lax.broadcasted_iota(jnp.int32, (1, LANES), 1)

    def consume(b, c, n_groups, slot):
        qf = q_ref[b]
        qb = (qf * QK_SCALE_64).astype(BF16)

        @pl.when(c == 0)
        def _():
            km_ref[...] = jnp.zeros(km_ref.shape, F32)
            m_ref[...] = jnp.zeros(m_ref.shape, F32)
            l_ref[...] = jnp.zeros(l_ref.shape, F32)

        kf_t =_pages(k_buf, slot, n_pages, F32, 1)
        v_t = _pages(v_buf, slot, n_pages, BF16, 1)
        s = jnp.dot(qb, kf_t.astype(BF16), preferred_element_type=F32)
        for j in range(bps):
            cols = slice(j * MOBA_BLOCK, (j + 1) * MOBA_BLOCK)
            here = lane == c * bps + j
            kmean = jnp.sum(kf_t[:, cols], axis=-1, keepdims=True) * (1.0 / MOBA_BLOCK)
            km_ref[...] = jnp.where(here, kmean, km_ref[...])
            sj = s[:, cols]
            mj = jnp.max(sj, axis=-1, keepdims=True)
            pj = jnp.exp(sj - mj)
            lj = jnp.sum(pj, axis=-1, keepdims=True)
            m_ref[...] = jnp.where(here, mj, m_ref[...])
            l_ref[...] = jnp.where(here, lj, l_ref[...])
            oblk_ref[c * bps + j] = lax.dot_general(pj.astype(BF16), v_t[:, cols], NT_DIMS,
                                                    preferred_element_type=F32)

        @pl.when(c == n_groups - 1)
        def _():
            gs = jnp.dot(qf, km_ref[...], precision=lax.Precision.HIGHEST,
                         preferred_element_type=F32)
            sel = _top3_mask(gs, lane, n_past_blocks, 1)
            s_new = jnp.sum(qf * QK_SCALE_64 * knew_ref[b], axis=-1, keepdims=True)
            mb = m_ref[...]
            m_all = jnp.maximum(jnp.max(jnp.where(sel > 0.0, mb, -jnp.inf), axis=-1, keepdims=True),
                                s_new)
            coef = jnp.where(sel > 0.0, jnp.exp(mb - m_all), 0.0)
            e_new = jnp.exp(s_new - m_all)
            denom = jnp.sum(coef * l_ref[...], axis=-1, keepdims=True) + e_new
            o = e_new * vnew_ref[b]
            for n in range(n_past_blocks):
                o = o + coef[:, n:n + 1] * oblk_ref[n]
            o_ref[b] = o / denom

    _paged_loop(pt_ref, layer, (k_hbm, v_hbm), (k_buf, v_buf), sems, n_pages, consume)


def moba_decode(page_table, q, knew, vnew, cache_k_t, cache_v_t, layer):
    nb, hp = q.shape[:2]
    nblk = page_table.shape[1] * PAGE // MOBA_BLOCK
    assert nblk <= LANES
    return _decode_call(
        functools.partial(_moba_decode_kernel, layer=layer, n_pages=PAGES_PER_STEP), "moba_decode",
        page_table, [q, knew, vnew], [cache_k_t, cache_v_t],
        jax.ShapeDtypeStruct((nb, hp, MOBA_HD), F32),
        [pltpu.VMEM((MOBA_HD, LANES), F32), pltpu.VMEM((hp, LANES), F32),
         pltpu.VMEM((hp, LANES), F32), pltpu.VMEM((nblk, hp, MOBA_HD), F32)])


def _rope_tables(pos):
    def blocks(rot_dim):
        half = rot_dim // 2
        inv = ROPE_THETA ** (-jnp.arange(half, dtype=F32) / half)
        ang = pos.astype(F32)[:, None] * inv[None, :]
        cos, sin = jnp.cos(ang), jnp.sin(ang)
        n = pos.shape[0]
        rest = 64 - rot_dim
        one, zero = jnp.ones((n, rest), F32), jnp.zeros((n, rest + half), F32)
        c = jnp.concatenate([cos, cos, one], axis=1)
        sa = jnp.concatenate([-sin, zero], axis=1)
        sb = jnp.concatenate([zero[:, :half], sin, zero[:, :rest]], axis=1)
        return [jnp.tile(t, (1, 2)) for t in (c, sa, sb)]
    return jnp.concatenate(blocks(16) + blocks(64), axis=1)


def _row_tile(n, target):
    return max(t for t in range(16, min(n, target) + 1, 16) if n % t == 0)


def _pad_axis(a, axis, size):
    pad = [(0, 0)] * a.ndim
    pad[axis] = (0, size - a.shape[axis])
    return jnp.pad(a, pad)


def _query_tiles(x, batch, nq, tq, heads, feat):
    x = x.reshape(batch, nq, tq, heads, feat)
    return jnp.transpose(x, (0, 1, 4, 3, 2)).reshape(batch, nq, feat, heads * tq)


def _untile_queries(x, batch, nq, tq, heads, feat):
    x = x.reshape(batch, nq, feat, heads, tq)
    return jnp.transpose(x, (0, 1, 4, 3, 2)).reshape(batch * nq * tq, heads * feat)


def _key_tiles(x, batch, nk, tk):
    x = x.reshape(batch, nk, tk, x.shape[-1])
    return x, jnp.swapaxes(x, 2, 3)


def kernel(x_prompt, x_sample, cache_mla_ckv, cache_mla_krope, cache_moba_k, cache_moba_v,
           cache_diff_k, cache_diff_v, page_table, ffn1_norm_g, ffn1_w_gate, ffn1_w_up, ffn1_w_down,
           mix_norm_g, w_in, mla_q_norm_g, mla_w_uq, mla_kv_norm_g, mla_w_ukv,
           diff_lambda_q1, diff_lambda_k1, diff_lambda_q2, diff_lambda_k2, diff_subln_g, w_out,
           ffn2_norm_g, ffn2_w_gate, ffn2_w_up, ffn2_w_down, final_norm_g):
    batch, seq, d_model = x_prompt.shape
    dec_b = x_sample.shape[0]
    depth = w_in.shape[0]
    n_prompt = batch * seq
    n_rows = n_prompt + dec_b
    past_len = page_table.shape[1] * PAGE
    n_pool = cache_mla_ckv.shape[1]
    tq, tk = Q_TILE, min(K_TILE, seq)
    nq, nk, nblk = seq // tq, seq // tk, seq // MOBA_BLOCK

    tm_big, tm_mid, tm_small = (_row_tile(n_rows, t) for t in (1040, 520, 320))

    x = jnp.concatenate([x_prompt.reshape(n_prompt, d_model), x_sample.reshape(dec_b, d_model)], axis=0)
    pos = jnp.concatenate([jnp.tile(jnp.arange(seq), batch), jnp.full((dec_b,), past_len)])
    tab = _rope_tables(pos)

    ck_mla = jnp.transpose(cache_mla_ckv, (0, 1, 3, 2))
    ck_kr = jnp.transpose(cache_mla_krope, (0, 1, 3, 2))
    ck_mk = jnp.transpose(cache_moba_k, (0, 1, 3, 4, 2)).reshape(depth, n_pool, MOBA_HD, PAGE)
    ck_mv = jnp.transpose(cache_moba_v, (0, 1, 3, 4, 2)).reshape(depth, n_pool, MOBA_HD, PAGE)
    ck_dk = jnp.transpose(cache_diff_k, (0, 1, 3, 4, 5, 2)).reshape(depth, n_pool, 2 * DIFF_D, PAGE)
    ck_dv = cache_diff_v.reshape(depth, n_pool, PAGE, 2 * DIFF_D)

    ffn1_w = [w.astype(BF16) for w in (ffn1_w_gate, ffn1_w_up, ffn1_w_down)]
    ffn2_w = [w.astype(BF16) for w in (ffn2_w_gate, ffn2_w_up, ffn2_w_down)]
    w_in_b = _pad_axis(w_in, 2, 4096).astype(BF16)
    w_out_b = w_out.astype(BF16)

    def ffn(x, g, w, layer):
        xn = rmsnorm(x, g, BF16, tm_small)
        h = ffn_gateup(xn, w[0], w[1], layer, tm_big, 512)
        return ffn_down(h, w[2], x, layer, tm_mid, 512)

    rows_out = [[] for _ in range(6)]
    for l in range(depth):
        x = ffn(x, ffn1_norm_g[l], ffn1_w, l)

        hn = rmsnorm(x, mix_norm_g[l], BF16, tm_small)
        z = matmul_f32out(hn, w_in_b, l, tm_big, 512)

        wuq = mla_w_uq[l].reshape(-1, MLA_HEADS, MLA_NOPE + MLA_ROPE)
        wuq_rope = jnp.pad(wuq[:, :, MLA_NOPE:], ((0, 0), (0, 0), (64, 0)))
        wuq_p = jnp.concatenate([wuq[:, :, :MLA_NOPE].reshape(-1, MLA_HEADS * 128),
                                 wuq_rope.reshape(-1, MLA_HEADS * 128)], axis=1).astype(BF16)
        wukv = mla_w_ukv[l].reshape(MLA_LORA, MLA_HEADS, MLA_NOPE + 128)
        wk = _pad_axis(jnp.transpose(wukv[:, :, :MLA_NOPE], (1, 2, 0)), 2, 256).astype(BF16)
        wv = _pad_axis(jnp.transpose(wukv[:, :, MLA_NOPE:], (1, 0, 2)), 1, 256).astype(BF16)
        wv_t = jnp.swapaxes(wv, 1, 2)
        gq = mla_q_norm_g[l].reshape(1, -1)
        gkv = _pad_axis(mla_kv_norm_g[l].reshape(1, -1), 1, 256)

        qm, kv, mq, mkv, dq, dkv = in_proj_post(z, tab, gq, gkv, wuq_p, wk, tm_small)
        r_mk, r_mv = mkv[:, :MOBA_HD], mkv[:, MOBA_HD:]
        lam_vecs = jnp.stack([diff_lambda_q1[l], diff_lambda_k1[l], diff_lambda_q2[l], diff_lambda_k2[l]])
        lam_init = 0.8 - 0.6 * math.exp(-0.3 * l)
        g_sub = diff_subln_g[l].reshape(1, -1)

        mla_k, mla_kt = _key_tiles(kv[:n_prompt].astype(BF16), batch, nk, tk)
        oa_p = mla_prompt(_query_tiles(qm[:n_prompt], batch, nq, tq, MLA_HEADS, 256),
                          mla_k, mla_kt, wv_t, tq, tk)
        oa_p = jnp.transpose(oa_p.reshape(batch, nq, MLA_HEADS * 128, tq), (0, 1, 3, 2))
        oa_p = oa_p.reshape(n_prompt, MLA_HEADS * 128)

        diff_k, _ = _key_tiles(dkv[:n_prompt, :128].astype(BF16), batch, nk, tk)
        _, diff_vt = _key_tiles(dkv[:n_prompt, 128:].astype(BF16), batch, nk, tk)
        oc_p = diff_prompt(_query_tiles(dq[:n_prompt], batch, nq, tq, DIFF_HEADS, 128),
                           diff_k, diff_vt, lam_vecs, g_sub.reshape(-1, 1), lam_init, tq, tk)
        oc_p = _untile_queries(oc_p, batch, nq, tq, DIFF_HEADS, 128)

        moba_k, _ = _key_tiles(r_mk[:n_prompt], batch, nblk, MOBA_BLOCK)
        _, moba_vt = _key_tiles(r_mv[:n_prompt].astype(BF16), batch, nblk, MOBA_BLOCK)
        ob_p = moba_prompt(_query_tiles(mq[:n_prompt], batch, nq, tq, MOBA_HEADS, MOBA_HD),
                           moba_k, moba_vt, tq)
        ob_p = _untile_queries(ob_p, batch, nq, tq, MOBA_HEADS, MOBA_HD)

        qm_d = _pad_axis(qm[n_prompt:].reshape(dec_b, MLA_HEADS, 256), 1, 16)
        oa_d = mla_decode(page_table, qm_d, kv[n_prompt:].reshape(dec_b, 1, 256), wv,
                          ck_mla, ck_kr, l).reshape(dec_b, MLA_HEADS * 128)
        dq_d = _pad_axis(dq[n_prompt:].reshape(dec_b, DIFF_HEADS, 128), 1, 16)
        oc_d = diff_decode(page_table, dq_d, dkv[n_prompt:, :128].reshape(dec_b, 1, 128),
                           dkv[n_prompt:, 128:].reshape(dec_b, 1, 128), lam_vecs, g_sub, lam_init,
                           ck_dk, ck_dv, l)
        oc_d = oc_d[:, :DIFF_HEADS].reshape(dec_b, DIFF_HEADS * 128)
        mq_d = _pad_axis(mq[n_prompt:].reshape(dec_b, MOBA_HEADS, MOBA_HD), 1, 32)
        ob_d = moba_decode(page_table, mq_d, r_mk[n_prompt:].reshape(dec_b, 1, MOBA_HD),
                           r_mv[n_prompt:].reshape(dec_b, 1, MOBA_HD), ck_mk, ck_mv, l)
        ob_d = ob_d[:, :MOBA_HEADS].reshape(dec_b, MOBA_HEADS * MOBA_HD)

        mixed = jnp.concatenate([jnp.concatenate([oa_p, ob_p, oc_p], axis=1),
                                 jnp.concatenate([oa_d, ob_d, oc_d], axis=1).astype(BF16)], axis=0)
        x = out_proj(mixed, w_out_b, x, l, tm_big, 512)

        x = ffn(x, ffn2_norm_g[l], ffn2_w, l)

        for acc, r in zip(rows_out, (kv[:, :MLA_LORA], kv[:, MLA_LORA:], r_mk, r_mv,
                                     dkv[:, :128], dkv[:, 128:])):
            acc.append(r)

    y = rmsnorm(x, final_norm_g, F32, tm_small)
    y_prompt = y[:n_prompt].reshape(batch, seq, d_model)
    y_sample = y[n_prompt:].reshape(dec_b, 1, d_model)

    tails = ((MLA_LORA,), (MLA_ROPE,), (1, MOBA_HD), (1, MOBA_HD), (1, 2, DIFF_D), (1, 2 * DIFF_D))
    stacked = [jnp.stack(a) for a in rows_out]
    prompt_rows = tuple(s[:, :n_prompt].reshape((depth, batch, seq) + t) for s, t in zip(stacked, tails))
    sample_rows = tuple(s[:, n_prompt:].reshape((depth, dec_b, 1) + t) for s, t in zip(stacked, tails))
    return (y_prompt, y_sample) + prompt_rows + sample_rows
```
